```python
import jax, jax.numpy as jnp
from jax import lax
import numpy as np

D_MODEL = 1024
BATCH = 32
SEQ = 2048
DEPTH = 1

CHUNK = 64
MEM_LEN = 256
MIX_WIDTH = D_MODEL
CONV_WIDTH = MIX_WIDTH // 2
CONV_K = 31
RWKV_WIDTH = MIX_WIDTH - CONV_WIDTH
RWKV_HEAD = 64
RWKV_HEADS = RWKV_WIDTH // RWKV_HEAD
DECAY_LORA = 32
AAA_LORA = 32
GATE_LORA = 96
N_XHEADS = 4
XHEAD_DIM = D_MODEL // N_XHEADS
D_FF = 4 * D_MODEL
RMS_EPS = 1e-6
LN_EPS = 1e-5
GN_EPS = 1e-5 * RWKV_HEAD
RWKV_COLS = 3 * RWKV_WIDTH + DECAY_LORA + AAA_LORA + GATE_LORA
IN_COLS = 2 * CONV_WIDTH + RWKV_COLS

kernel_name = 'hybrid_conformer_rwkv7_stream_block'


def rmsnorm(x, g):
    xf = x.astype(jnp.float32)
    y = xf * lax.rsqrt(jnp.mean(xf * xf, axis=-1, keepdims=True) + RMS_EPS)
    return (y * g.astype(jnp.float32)).astype(x.dtype)


def conv_group(pa, conv_w, conv_b, ln_g, ln_b):
    u = pa[..., :CONV_WIDTH] * jax.nn.sigmoid(pa[..., CONV_WIDTH:])
    u = lax.conv_general_dilated(
        u, conv_w[:, None, :].astype(u.dtype), window_strides=(1,),
        padding=[(CONV_K - 1, 0)],
        dimension_numbers=('NWC', 'WIO', 'NWC'),
        feature_group_count=CONV_WIDTH) + conv_b
    uf = u.astype(jnp.float32)
    mu = jnp.mean(uf, axis=-1, keepdims=True)
    var = jnp.mean(jnp.square(uf - mu), axis=-1, keepdims=True)
    un = (uf - mu) * lax.rsqrt(var + LN_EPS) * ln_g + ln_b
    return jax.nn.silu(un).astype(pa.dtype)


def wkv7_scan(r, w, k, v, kk, a):
    B, S, H, N = r.shape

    def to_chunks(z):
        return z.transpose(1, 0, 2, 3).reshape(S // CHUNK, CHUNK, B, H, N)

    def step(state, inp):
        r_t, w_t, k_t, v_t, kk_t, a_t = inp
        sa = jnp.einsum('bhvk,bhk->bhv', state, -kk_t)
        state = (state * w_t[:, :, None, :]
                 + sa[..., None] * (kk_t * a_t)[:, :, None, :]
                 + v_t[..., None] * k_t[:, :, None, :])
        return state, jnp.einsum('bhvk,bhk->bhv', state, r_t)

    def chunk_step(state, inp):
        return lax.scan(step, state, inp)

    s0 = jnp.zeros((B, H, N, N), jnp.float32)
    _, y = lax.scan(chunk_step, s0, tuple(to_chunks(z) for z in (r, w, k, v, kk, a)))
    return y.reshape(S, B, H, N).transpose(1, 0, 2, 3)


def rwkv7_group(pb, mu_b, w0, w_decay2, a0, a_lora2, g_lora2, k_k, k_a, r_k, lnx_g, lnx_b):
    B, S, _ = pb.shape
    H, N, C = RWKV_HEADS, RWKV_HEAD, RWKV_WIDTH
    prev = jnp.pad(pb, ((0, 0), (1, 0), (0, 0)))[:, :-1]
    z = (pb + mu_b * (prev - pb)).astype(jnp.float32)
    r = z[..., :C]
    k = z[..., C:2 * C]
    v = z[..., 2 * C:3 * C]
    o = 3 * C
    zw = z[..., o:o + DECAY_LORA]
    za = z[..., o + DECAY_LORA:o + DECAY_LORA + AAA_LORA]
    zg = z[..., o + DECAY_LORA + AAA_LORA:]
    w_log = -jax.nn.softplus(-(w0 + jnp.tanh(zw) @ w_decay2)) - 0.5
    decay = jnp.exp(-jnp.exp(w_log))
    a = jax.nn.sigmoid(a0 + za @ a_lora2)
    g = jax.nn.sigmoid(zg) @ g_lora2

    def heads(t):
        return t.reshape(B, S, H, N)

    kk = heads(k * k_k)
    kk = kk / jnp.maximum(jnp.sqrt(jnp.sum(kk * kk, axis=-1, keepdims=True)), 1e-12)
    k = k * (1.0 + (a - 1.0) * k_a)
    rh, kh, vh = heads(r), heads(k), heads(v)
    y = wkv7_scan(rh, heads(decay), kh, vh, kk, heads(a))
    mu = jnp.mean(y, axis=-1, keepdims=True)
    var = jnp.mean(jnp.square(y - mu), axis=-1, keepdims=True)
    y = ((y - mu) * lax.rsqrt(var + GN_EPS)).reshape(B, S, C) * lnx_g + lnx_b
    bonus = (jnp.sum(rh * kh * r_k.reshape(H, N), axis=-1, keepdims=True) * vh).reshape(B, S, C)
    return ((y + bonus) * g).astype(pb.dtype)


def cross_attention(h, mem_n, wq, wk, wv, wo):
    B, S, _ = h.shape
    M = mem_n.shape[1]
    q = (h @ wq).reshape(B, S, N_XHEADS, XHEAD_DIM)
    k = (mem_n @ wk).reshape(B, M, N_XHEADS, XHEAD_DIM)
    v = (mem_n @ wv).reshape(B, M, N_XHEADS, XHEAD_DIM)
    s = jnp.einsum('bshd,bmhd->bhsm', q, k).astype(jnp.float32) * (XHEAD_DIM ** -0.5)
    p = jax.nn.softmax(s, axis=-1).astype(h.dtype)
    out = jnp.einsum('bhsm,bmhd->bshd', p, v).reshape(B, S, D_MODEL)
    return out @ wo


def setup_inputs(seed: int = 0) -> dict:
    key = jax.random.key(seed)
    ks = jax.random.split(key, 32)
    L, D = DEPTH, D_MODEL
    f32 = jnp.float32

    def nrm(k, shape, scale):
        return jax.random.normal(k, shape, f32) * scale

    def gain(k, shape):
        return 1.0 + 0.02 * jax.random.normal(k, shape, f32)

    return {
        'x': jax.random.normal(ks[0], (BATCH, SEQ, D), f32),
        'mem': jax.random.normal(ks[1], (BATCH, MEM_LEN, D), f32),
        'g_mix': gain(ks[2], (L, D)),
        'w_in': nrm(ks[3], (L, D, IN_COLS), D ** -0.5),
        'conv_w': nrm(ks[4], (L, CONV_K, CONV_WIDTH), CONV_K ** -0.5),
        'conv_b': nrm(ks[5], (L, CONV_WIDTH), 0.02),
        'conv_ln_g': gain(ks[6], (L, CONV_WIDTH)),
        'conv_ln_b': nrm(ks[7], (L, CONV_WIDTH), 0.02),
        'mu_b': jax.random.uniform(ks[8], (L, RWKV_COLS), f32, 0.0, 1.0),
        'w0': jax.random.uniform(ks[9], (L, RWKV_WIDTH), f32, -6.0, 1.0),
        'w_decay2': nrm(ks[10], (L, DECAY_LORA, RWKV_WIDTH), 0.1 * DECAY_LORA ** -0.5),
        'a0': nrm(ks[11], (L, RWKV_WIDTH), 0.1),
        'a_lora2': nrm(ks[12], (L, AAA_LORA, RWKV_WIDTH), 0.5 * AAA_LORA ** -0.5),
        'g_lora2': nrm(ks[13], (L, GATE_LORA, RWKV_WIDTH), GATE_LORA ** -0.5),
        'k_k': 0.85 + nrm(ks[14], (L, RWKV_WIDTH), 0.02),
        'k_a': 1.0 + nrm(ks[15], (L, RWKV_WIDTH), 0.02),
        'r_k': nrm(ks[16], (L, RWKV_WIDTH), 0.1),
        'lnx_g': gain(ks[17], (L, RWKV_WIDTH)),
        'lnx_b': nrm(ks[18], (L, RWKV_WIDTH), 0.02),
        'w_out': nrm(ks[19], (L, MIX_WIDTH, D), MIX_WIDTH ** -0.5),
        'g_cross': gain(ks[20], (L, D)),
        'g_mem': gain(ks[21], (L, D)),
        'wq': nrm(ks[22], (L, D, D), D ** -0.5),
        'wk': nrm(ks[23], (L, D, D), D ** -0.5),
        'wv': nrm(ks[24], (L, D, D), D ** -0.5),
        'wo': nrm(ks[25], (L, D, D), D ** -0.5),
        'g_ffn': gain(ks[26], (L, D)),
        'w_ff1': nrm(ks[27], (L, D, D_FF), D ** -0.5),
        'w_ff2': nrm(ks[28], (L, D_FF, D), D_FF ** -0.5),
        'g_final': gain(ks[29], (D,)),
    }


def reference(x, mem, g_mix, w_in, conv_w, conv_b, conv_ln_g, conv_ln_b, mu_b, w0,
              w_decay2, a0, a_lora2, g_lora2, k_k, k_a, r_k, lnx_g, lnx_b, w_out,
              g_cross, g_mem, wq, wk, wv, wo, g_ffn, w_ff1, w_ff2, g_final):
    for l in range(DEPTH):
        h = rmsnorm(x, g_mix[l])
        p = h @ w_in[l]
        y_conv = conv_group(p[..., :2 * CONV_WIDTH], conv_w[l], conv_b[l],
                            conv_ln_g[l], conv_ln_b[l])
        y_rwkv = rwkv7_group(p[..., 2 * CONV_WIDTH:], mu_b[l], w0[l], w_decay2[l], a0[l],
                             a_lora2[l], g_lora2[l], k_k[l], k_a[l], r_k[l],
                             lnx_g[l], lnx_b[l])
        x = x + jnp.concatenate([y_conv, y_rwkv], axis=-1) @ w_out[l]
        x = x + cross_attention(rmsnorm(x, g_cross[l]), rmsnorm(mem, g_mem[l]),
                                wq[l], wk[l], wv[l], wo[l])
        hf = rmsnorm(x, g_ffn[l])
        x = x + jnp.square(jax.nn.relu(hf @ w_ff1[l])) @ w_ff2[l]
    return rmsnorm(x, g_final)
```

```python
import functools

import jax
import jax.numpy as jnp
from jax import lax
from jax.experimental import pallas as pl
from jax.experimental.pallas import tpu as pltpu

F32 = jnp.float32
BF16 = jnp.bfloat16

D_MODEL = 1024
CONV_W = 512
CONV_K = 31
RW = 512
HEAD = 64
LORA_PAD = 256
DEC_L, AAA_L, GATE_L = 32, 32, 96
P_COLS = 2 * CONV_W + 3 * RW + LORA_PAD
RWKV_P = 3 * RW + LORA_PAD
N_XH = 4
XHD = D_MODEL // N_XH
MEM_LEN = 256
D_FF = 4 * D_MODEL
RMS_EPS = 1e-6
LN_EPS = 1e-5
GN_EPS = 1e-5 * HEAD

CH = 64
GRP = 4
GW = GRP * HEAD
TT = 256
HALO = 32
CONV_RB = 32
TQ = 512
TM = 512
FF_CH = 1024
VMEM_LIMIT = 56 * 1024 * 1024


def _bdot(a, b):
    return jnp.dot(a.astype(BF16), b.astype(BF16), preferred_element_type=F32)


def _bdot_nt(a, b):
    return lax.dot_general(a.astype(BF16), b.astype(BF16), (((1,), (1,)), ((), ())),
                           preferred_element_type=F32)


def _bdot_tn(a, b):
    return lax.dot_general(a.astype(BF16), b.astype(BF16), (((0,), (0,)), ((), ())),
                           preferred_element_type=F32)


def _rms(x, g):
    return x * lax.rsqrt(jnp.mean(x * x, axis=-1, keepdims=True) + RMS_EPS) * g


def _seg_sum(x, ones_bd):
    hi = x.astype(BF16)
    lo = (x - hi.astype(F32)).astype(BF16)
    outs = []
    for g in range(RW // GW):
        sl = slice(g * GW, (g + 1) * GW)
        outs.append(jnp.dot(hi[:, sl], ones_bd, preferred_element_type=F32)
                    + jnp.dot(lo[:, sl], ones_bd, preferred_element_type=F32))
    return jnp.concatenate(outs, axis=1)


def _block_diag(z, blk):
    return jnp.concatenate([z] * GRP, axis=0) * blk


def _mixer_kernel(x_ref, gmix_ref, win_ref, cw_ref, cb_ref, clg_ref, clb_ref, mu_ref, w0_ref, wd_ref,
                  a0_ref, wa_ref, wg_ref, kkw_ref, ka_ref, rk_ref, lg_ref, lb_ref, wout_ref,
                  tri_ref, ones_ref, blk_ref,
                  o_ref,
                  p_sc, u_sc, conv_sc, strm_sc, gam_sc, y_sc, s_sc):
    j = pl.program_id(1)

    @pl.when(j == 0)
    def _():
        p_sc[0:8, :] = jnp.zeros((8, P_COLS), F32)
        u_sc[0:HALO, :] = jnp.zeros((HALO, CONV_W), F32)
        s_sc[...] = jnp.zeros(s_sc.shape, F32)

    x = x_ref[0]
    h = _rms(x, gmix_ref[...])
    p_sc[8:8 + TT, :] = _bdot(h, win_ref[...])

    u_sc[HALO:HALO + TT, :] = p_sc[8:8 + TT, 0:CONV_W] * jax.nn.sigmoid(p_sc[8:8 + TT, CONV_W:2 * CONV_W])
    off = HALO - (CONV_K - 1)
    for rb in range(TT // CONV_RB):
        base = rb * CONV_RB
        acc = jnp.broadcast_to(cb_ref[...], (CONV_RB, CONV_W))
        for t in range(CONV_K):
            acc = acc + cw_ref[t:t + 1, :] * u_sc[off + base + t:off + base + t + CONV_RB, :]
        conv_sc[base:base + CONV_RB, :] = acc
    u_sc[0:HALO, :] = u_sc[TT:TT + HALO, :]
    uc = conv_sc[...]
    m = jnp.mean(uc, axis=-1, keepdims=True)
    dlt = uc - m
    var = jnp.mean(dlt * dlt, axis=-1, keepdims=True)
    y_conv = jax.nn.silu(dlt * lax.rsqrt(var + LN_EPS) * clg_ref[...] + clb_ref[...])

    pb = p_sc[8:8 + TT, 2 * CONV_W:]
    prev = p_sc[7:7 + TT, 2 * CONV_W:]
    z = pb + mu_ref[...] * (prev - pb)
    p_sc[7:8, :] = p_sc[7 + TT:8 + TT, :]
    r = z[:, 0:RW]
    k = z[:, RW:2 * RW]
    v = z[:, 2 * RW:3 * RW]
    zl = z[:, 3 * RW:]
    dec = _bdot(jnp.tanh(zl), wd_ref[...])
    w_log = -jax.nn.softplus(-(w0_ref[...] + dec)) - 0.5
    lw = -jnp.exp(w_log)
    a = jax.nn.sigmoid(a0_ref[...] + _bdot(zl, wa_ref[...]))
    g = _bdot(jax.nn.sigmoid(zl), wg_ref[...])
    ones_bd = ones_ref[...]
    kk = k * kkw_ref[...]
    kk = kk / jnp.maximum(jnp.sqrt(_seg_sum(kk * kk, ones_bd)), 1e-12)
    k = k * (1.0 + (a - 1.0) * ka_ref[...])
    bonus = _seg_sum(r * k * rk_ref[...], ones_bd) * v

    l1 = lw.astype(BF16)
    l2 = (lw - l1.astype(F32)).astype(BF16)
    l3 = (lw - l1.astype(F32) - l2.astype(F32)).astype(BF16)
    tri = tri_ref[...]
    cs = (jnp.dot(tri, l1, preferred_element_type=F32) + jnp.dot(tri, l2, preferred_element_type=F32)
          + jnp.dot(tri, l3, preferred_element_type=F32))
    cl = jnp.concatenate(
        [jnp.broadcast_to(cs[c * CH + CH - 1:c * CH + CH, :], (CH, RW)) for c in range(TT // CH)], axis=0)
    e_in = jnp.exp(cs)
    e_inv = jnp.exp(-cs)
    e_end = jnp.exp(cl - cs)
    kka = kk * a
    strm_sc[0] = (kk * jnp.exp(cs - lw)).astype(BF16)
    strm_sc[1] = (r * e_in).astype(BF16)
    strm_sc[2] = (kka * e_inv).astype(BF16)
    strm_sc[3] = (k * e_inv).astype(BF16)
    strm_sc[4] = (kka * e_end).astype(BF16)
    strm_sc[5] = (k * e_end).astype(BF16)
    strm_sc[6] = v.astype(BF16)
    gam_sc[...] = jnp.exp(cl)

    blk = blk_ref[...]
    blk_f = blk.astype(F32)
    ri = lax.broadcasted_iota(jnp.int32, (CH, GW), 0)
    ci = lax.broadcasted_iota(jnp.int32, (CH, GW), 1) % CH
    tri_s = ri > ci
    tri_i = ri >= ci
    eye_cat = (ri == ci).astype(F32)

    def chunk_body(c, carry):
        rows = pl.ds(pl.multiple_of(c * CH, CH), CH)
        for gi in range(RW // GW):
            ln = slice(gi * GW, (gi + 1) * GW)
            kt = strm_sc[0, rows, ln]
            rt = strm_sc[1, rows, ln]
            bt = strm_sc[2, rows, ln]
            kq = strm_sc[3, rows, ln]
            bg = strm_sc[4, rows, ln]
            kg = strm_sc[5, rows, ln]
            vv = strm_sc[6, rows, ln]
            lhs = jnp.concatenate([kt, rt], axis=0)
            gb = _bdot_nt(lhs, _block_diag(bt, blk))
            gk = _bdot_nt(lhs, _block_diag(kq, blk))
            pm = jnp.where(tri_s, -gb[:CH], 0.0)
            lk = jnp.where(tri_s, gk[:CH], 0.0)
            arb = jnp.where(tri_i, gb[CH:], 0.0)
            ark = jnp.where(tri_i, gk[CH:], 0.0)
            w = eye_cat + pm
            q = _bdot(pm, _block_diag(pm.astype(BF16), blk))
            for _ in range(4):
                rr = _bdot(jnp.concatenate([w, q], axis=0), _block_diag(q.astype(BF16), blk))
                w = w + rr[:CH]
                q = rr[CH:]
            w = w + _bdot(w, _block_diag(q.astype(BF16), blk))
            xy = _bdot(jnp.concatenate([lk, ark], axis=0), _block_diag(vv, blk))
            st = s_sc[gi]
            zr = _bdot_nt(lhs, st)
            dm = -_bdot(w, _block_diag((xy[:CH] + zr[:CH]).astype(BF16), blk))
            dmb = dm.astype(BF16)
            y = zr[CH:] + xy[CH:] + _bdot(arb, _block_diag(dmb, blk))
            upd = _bdot_tn(jnp.concatenate([dmb, vv], axis=0), jnp.concatenate([bg, kg], axis=0))
            s_sc[gi] = gam_sc[pl.ds(pl.multiple_of(c * CH, CH), 1), ln] * st + upd * blk_f
            y_sc[rows, ln] = y
        return carry

    lax.fori_loop(0, TT // CH, chunk_body, 0)

    yw = y_sc[...]
    mu_h = _seg_sum(yw, ones_bd) * (1.0 / HEAD)
    dy = yw - mu_h
    var_h = _seg_sum(dy * dy, ones_bd) * (1.0 / HEAD)
    yn = dy * lax.rsqrt(var_h + GN_EPS) * lg_ref[...] + lb_ref[...]
    y_rwkv = (yn + bonus) * g

    o_ref[0] = x + _bdot(y_conv, wout_ref[0:CONV_W, :]) + _bdot(y_rwkv, wout_ref[CONV_W:, :])


def _const_spec(shape):
    nd = len(shape)
    return pl.BlockSpec(shape, lambda *_: (0,) * nd)


def _mixer(x, gmix, win, cw, cb, clg, clb, mu, w0, wd, a0, wa, wg, kkw, ka, rk, lg, lb, wout, tri, ones_bd, blk):
    B, S, D = x.shape
    consts = (gmix, win, cw, cb, clg, clb, mu, w0, wd, a0, wa, wg, kkw, ka, rk, lg, lb, wout, tri, ones_bd, blk)
    return pl.pallas_call(
        _mixer_kernel,
        grid=(B, S // TT),
        in_specs=[pl.BlockSpec((1, TT, D), lambda b, j: (b, j, 0))] + [_const_spec(c.shape) for c in consts],
        out_specs=pl.BlockSpec((1, TT, D), lambda b, j: (b, j, 0)),
        out_shape=jax.ShapeDtypeStruct((B, S, D), F32),
        scratch_shapes=[
            pltpu.VMEM((8 + TT, P_COLS), F32),
            pltpu.VMEM((HALO + TT, CONV_W), F32),
            pltpu.VMEM((TT, CONV_W), F32),
            pltpu.VMEM((7, TT, RW), BF16),
            pltpu.VMEM((TT, RW), F32),
            pltpu.VMEM((TT, RW), F32),
            pltpu.VMEM((RW // GW, GW, GW), F32),
        ],
        compiler_params=pltpu.CompilerParams(
            dimension_semantics=("arbitrary", "arbitrary"), vmem_limit_bytes=VMEM_LIMIT),
        name="mixer",
    )(x, *consts)


def _xattn_kernel(x_ref, mem_ref, gc_ref, gm_ref, wq_ref, wk_ref, wv_ref, wo_ref, o_ref, k_sc, v_sc):
    @pl.when(pl.program_id(1) == 0)
    def _():
        mn = _rms(mem_ref[0], gm_ref[...]).astype(BF16)
        k_sc[...] = jnp.dot(mn, wk_ref[...], preferred_element_type=F32).astype(BF16)
        v_sc[...] = jnp.dot(mn, wv_ref[...], preferred_element_type=F32).astype(BF16)

    x = x_ref[0]
    q = _bdot(_rms(x, gc_ref[...]), wq_ref[...])
    outs = []
    for hd in range(N_XH):
        sl = slice(hd * XHD, (hd + 1) * XHD)
        s = _bdot_nt(q[:, sl], k_sc[:, sl]) * (XHD ** -0.5)
        s = s - jnp.max(s, axis=-1, keepdims=True)
        e = jnp.exp(s)
        p = e / jnp.sum(e, axis=-1, keepdims=True)
        outs.append(_bdot(p, v_sc[:, sl]))
    o_ref[0] = x + _bdot(jnp.concatenate(outs, axis=1), wo_ref[...])


def _xattn(x, mem, gc, gm, wq, wk, wv, wo):
    B, S, D = x.shape
    M = mem.shape[1]
    consts = (gc, gm, wq, wk, wv, wo)
    return pl.pallas_call(
        _xattn_kernel,
        grid=(B, S // TQ),
        in_specs=[pl.BlockSpec((1, TQ, D), lambda b, j: (b, j, 0)),
                  pl.BlockSpec((1, M, D), lambda b, j: (b, 0, 0))] + [_const_spec(c.shape) for c in consts],
        out_specs=pl.BlockSpec((1, TQ, D), lambda b, j: (b, j, 0)),
        out_shape=jax.ShapeDtypeStruct((B, S, D), F32),
        scratch_shapes=[pltpu.VMEM((M, D), BF16), pltpu.VMEM((M, D), BF16)],
        compiler_params=pltpu.CompilerParams(
            dimension_semantics=("arbitrary", "arbitrary"), vmem_limit_bytes=VMEM_LIMIT),
        name="xattn",
    )(x, mem, *consts)


def _ffn_kernel(x_ref, gf_ref, w1_ref, w2_ref, gfin_ref, o_ref, *, final_norm):
    x = x_ref[...]
    h = _rms(x, gf_ref[...]).astype(BF16)
    acc = x
    for c in range(D_FF // FF_CH):
        sl = slice(c * FF_CH, (c + 1) * FF_CH)
        t = jnp.maximum(jnp.dot(h, w1_ref[:, sl], preferred_element_type=F32), 0.0)
        acc = acc + _bdot(t * t, w2_ref[sl, :])
    if final_norm:
        acc = _rms(acc, gfin_ref[...])
    o_ref[...] = acc


def _ffn(x2d, gf, w1, w2, gfin, final_norm):
    M, D = x2d.shape
    consts = (gf, w1, w2, gfin)
    return pl.pallas_call(
        functools.partial(_ffn_kernel, final_norm=final_norm),
        grid=(M // TM,),
        in_specs=[pl.BlockSpec((TM, D), lambda i: (i, 0))] + [_const_spec(c.shape) for c in consts],
        out_specs=pl.BlockSpec((TM, D), lambda i: (i, 0)),
        out_shape=jax.ShapeDtypeStruct((M, D), F32),
        compiler_params=pltpu.CompilerParams(
            dimension_semantics=("arbitrary",), vmem_limit_bytes=VMEM_LIMIT),
        name="ffn",
    )(x2d, *consts)


def _row(v):
    return v.reshape(1, -1).astype(F32)


def _mixer_constants():
    i = jnp.arange(TT)
    tri = ((i[:, None] // CH == i[None, :] // CH) & (i[:, None] >= i[None, :])).astype(BF16)
    jj = jnp.arange(GW)
    blk = (jj[:, None] // HEAD == jj[None, :] // HEAD).astype(BF16)
    return tri, blk, blk


def kernel(x, mem, g_mix, w_in, conv_w, conv_b, conv_ln_g, conv_ln_b, mu_b, w0, w_decay2, a0, a_lora2,
           g_lora2, k_k, k_a, r_k, lnx_g, lnx_b, w_out, g_cross, g_mem, wq, wk, wv, wo, g_ffn, w_ff1, w_ff2,
           g_final):
    B, S, D = x.shape
    depth = w_in.shape[0]
    tri, ones_bd, blk = _mixer_constants()
    pad_cols = LORA_PAD - (DEC_L + AAA_L + GATE_L)
    for l in range(depth):
        win = jnp.pad(w_in[l], ((0, 0), (0, pad_cols))).astype(BF16)
        mu = _row(jnp.pad(mu_b[l], (0, pad_cols)))
        wd = jnp.zeros((LORA_PAD, RW), F32).at[0:DEC_L].set(w_decay2[l]).astype(BF16)
        wa = jnp.zeros((LORA_PAD, RW), F32).at[DEC_L:DEC_L + AAA_L].set(a_lora2[l]).astype(BF16)
        wg = jnp.zeros((LORA_PAD, RW), F32).at[DEC_L + AAA_L:DEC_L + AAA_L + GATE_L].set(g_lora2[l]).astype(BF16)
        x = _mixer(x, _row(g_mix[l]), win, conv_w[l].astype(F32), _row(conv_b[l]), _row(conv_ln_g[l]),
                   _row(conv_ln_b[l]), mu, _row(w0[l]), wd, _row(a0[l]), wa, wg, _row(k_k[l]), _row(k_a[l]),
                   _row(r_k[l]), _row(lnx_g[l]), _row(lnx_b[l]), w_out[l].astype(BF16), tri, ones_bd, blk)
        x = _xattn(x, mem, _row(g_cross[l]), _row(g_mem[l]), wq[l].astype(BF16), wk[l].astype(BF16),
                   wv[l].astype(BF16), wo[l].astype(BF16))
        x = _ffn(x.reshape(B * S, D), _row(g_ffn[l]), w_ff1[l].astype(BF16), w_ff2[l].astype(BF16),
                 _row(g_final), final_norm=(l == depth - 1)).reshape(B, S, D)
    return x
```

```python
import functools

import jax
import jax.numpy as jnp
from jax import lax
from jax.experimental import pallas as pl
from jax.experimental.pallas import tpu as pltpu

F32 = jnp.float32
BF16 = jnp.bfloat16

D_MODEL = 1024
CONV_W = 512
CONV_K = 31
RW = 512
HEAD = 64
LORA_PAD = 256
DEC_L, AAA_L, GATE_L = 32, 32, 96
P_COLS = 2 * CONV_W + 3 * RW + LORA_PAD
RWKV_P = 3 * RW + LORA_PAD
N_XH = 4
XHD = D_MODEL // N_XH
MEM_LEN = 256
D_FF = 4 * D_MODEL
RMS_EPS = 1e-6
LN_EPS = 1e-5
GN_EPS = 1e-5 * HEAD

CH = 64
GRP = 4
GW = GRP * HEAD
TT = 256
A_WIDTH = 8
HALO = 32
CONV_RB = 32
TQ = 512
TM = 512
FF_CH = 1024
VMEM_LIMIT = 56 * 1024 * 1024


def _bdot(a, b):
    return jnp.dot(a.astype(BF16), b.astype(BF16), preferred_element_type=F32)


def _bdot_nt(a, b):
    return lax.dot_general(a.astype(BF16), b.astype(BF16), (((1,), (1,)), ((), ())),
                           preferred_element_type=F32)


def _bdot_tn(a, b):
    return lax.dot_general(a.astype(BF16), b.astype(BF16), (((0,), (0,)), ((), ())),
                           preferred_element_type=F32)


def _rms(x, g):
    return x * lax.rsqrt(jnp.mean(x * x, axis=-1, keepdims=True) + RMS_EPS) * g


def _seg_sum(x, ones_bd):
    hi = x.astype(BF16)
    lo = (x - hi.astype(F32)).astype(BF16)
    outs = []
    for g in range(RW // GW):
        sl = slice(g * GW, (g + 1) * GW)
        outs.append(jnp.dot(hi[:, sl], ones_bd, preferred_element_type=F32)
                    + jnp.dot(lo[:, sl], ones_bd, preferred_element_type=F32))
    return jnp.concatenate(outs, axis=1)


def _block_diag(z, blk):
    return jnp.concatenate([z] * GRP, axis=0) * blk


def _interleave(a_chains, b_chains, width):
    pending = list(a_chains)
    live_a, live_b = [], list(b_chains)
    while pending or live_a or live_b:
        while pending and len(live_a) < width:
            live_a.append(pending.pop(0))
        nxt_a, nxt_b = [], []
        for group, nxt in ((live_a, nxt_a), (live_b, nxt_b)):
            for ch in group:
                try:
                    next(ch)
                    nxt.append(ch)
                except StopIteration:
                    pass
        live_a, live_b = nxt_a, nxt_b


def _mixer_kernel(x_ref, gmix_ref, win_ref, cw_ref, cb_ref, clg_ref, clb_ref, mu_ref, w0_ref, wd_ref,
                  a0_ref, wa_ref, wg_ref, kkw_ref, ka_ref, rk_ref, lg_ref, lb_ref, wout_ref,
                  tri_ref, ones_ref, blk_ref,
                  o_ref,
                  p_sc, u_sc, conv_sc, strm_sc, af_sc, ab_sc, bkt_sc, gcol_sc, gam_sc, y_sc, s_sc):
    j = pl.program_id(1)

    @pl.when(j == 0)
    def _():
        p_sc[0:8, :] = jnp.zeros((8, P_COLS), F32)
        u_sc[0:HALO, :] = jnp.zeros((HALO, CONV_W), F32)
        s_sc[...] = jnp.zeros(s_sc.shape, F32)

    x = x_ref[0]
    h = _rms(x, gmix_ref[...])
    p_sc[8:8 + TT, :] = _bdot(h, win_ref[...])

    u_sc[HALO:HALO + TT, :] = p_sc[8:8 + TT, 0:CONV_W] * jax.nn.sigmoid(p_sc[8:8 + TT, CONV_W:2 * CONV_W])
    off = HALO - (CONV_K - 1)
    for rb in range(TT // CONV_RB):
        base = rb * CONV_RB
        acc = jnp.broadcast_to(cb_ref[...], (CONV_RB, CONV_W))
        for t in range(CONV_K):
            acc = acc + cw_ref[t:t + 1, :] * u_sc[off + base + t:off + base + t + CONV_RB, :]
        conv_sc[base:base + CONV_RB, :] = acc
    u_sc[0:HALO, :] = u_sc[TT:TT + HALO, :]
    uc = conv_sc[...]
    m = jnp.mean(uc, axis=-1, keepdims=True)
    dlt = uc - m
    var = jnp.mean(dlt * dlt, axis=-1, keepdims=True)
    y_conv = jax.nn.silu(dlt * lax.rsqrt(var + LN_EPS) * clg_ref[...] + clb_ref[...])

    pb = p_sc[8:8 + TT, 2 * CONV_W:]
    prev = p_sc[7:7 + TT, 2 * CONV_W:]
    z = pb + mu_ref[...] * (prev - pb)
    p_sc[7:8, :] = p_sc[7 + TT:8 + TT, :]
    r = z[:, 0:RW]
    k = z[:, RW:2 * RW]
    v = z[:, 2 * RW:3 * RW]
    zl = z[:, 3 * RW:]
    dec = _bdot(jnp.tanh(zl), wd_ref[...])
    w_log = -jax.nn.softplus(-(w0_ref[...] + dec)) - 0.5
    lw = -jnp.exp(w_log)
    a = jax.nn.sigmoid(a0_ref[...] + _bdot(zl, wa_ref[...]))
    g = _bdot(jax.nn.sigmoid(zl), wg_ref[...])
    ones_bd = ones_ref[...]
    kk = k * kkw_ref[...]
    kk = kk / jnp.maximum(jnp.sqrt(_seg_sum(kk * kk, ones_bd)), 1e-12)
    k = k * (1.0 + (a - 1.0) * ka_ref[...])
    bonus = _seg_sum(r * k * rk_ref[...], ones_bd) * v

    l1 = lw.astype(BF16)
    l2 = (lw - l1.astype(F32)).astype(BF16)
    l3 = (lw - l1.astype(F32) - l2.astype(F32)).astype(BF16)
    tri = tri_ref[...]
    cs = (jnp.dot(tri, l1, preferred_element_type=F32) + jnp.dot(tri, l2, preferred_element_type=F32)
          + jnp.dot(tri, l3, preferred_element_type=F32))
    cl = jnp.concatenate(
        [jnp.broadcast_to(cs[c * CH + CH - 1:c * CH + CH, :], (CH, RW)) for c in range(TT // CH)], axis=0)
    e_in = jnp.exp(cs)
    e_inv = jnp.exp(-cs)
    e_end = jnp.exp(cl - cs)
    kka = kk * a
    strm_sc[0] = (kk * jnp.exp(cs - lw)).astype(BF16)
    strm_sc[1] = (r * e_in).astype(BF16)
    strm_sc[2] = (kka * e_inv).astype(BF16)
    strm_sc[3] = (k * e_inv).astype(BF16)
    strm_sc[4] = (kka * e_end).astype(BF16)
    strm_sc[5] = (k * e_end).astype(BF16)
    strm_sc[6] = v.astype(BF16)
    gam_sc[...] = jnp.exp(cl)

    blk = blk_ref[...]
    blk_f = blk.astype(F32)
    ri = lax.broadcasted_iota(jnp.int32, (CH, GW), 0)
    ci = lax.broadcasted_iota(jnp.int32, (CH, GW), 1) % CH
    tri_s = ri > ci
    tri_i = ri >= ci
    eye_cat = (ri == ci).astype(F32)

    n_grp = RW // GW
    n_chk = TT // CH

    done = set()

    def chain_a(c, gi):
        rows = slice(c * CH, (c + 1) * CH)
        ln = slice(gi * GW, (gi + 1) * GW)
        kt = strm_sc[0, rows, ln]
        lhs = jnp.concatenate([kt, strm_sc[1, rows, ln]], axis=0)
        gb = _bdot_nt(lhs, _block_diag(strm_sc[2, rows, ln], blk))
        gk = _bdot_nt(lhs, _block_diag(strm_sc[3, rows, ln], blk))
        yield
        pm = jnp.where(tri_s, -gb[:CH], 0.0)
        lk = jnp.where(tri_s, gk[:CH], 0.0)
        ab_sc[1, rows, ln] = jnp.where(tri_i, gb[CH:], 0.0).astype(BF16)
        ark = jnp.where(tri_i, gk[CH:], 0.0)
        w = eye_cat + pm
        q = _bdot(pm, _block_diag(pm.astype(BF16), blk))
        yield
        for _ in range(4):
            rr = _bdot(jnp.concatenate([w, q], axis=0), _block_diag(q.astype(BF16), blk))
            yield
            w = w + rr[:CH]
            q = rr[CH:]
        wq = _bdot(w, _block_diag(q.astype(BF16), blk))
        xy = _bdot(jnp.concatenate([lk, ark], axis=0), _block_diag(strm_sc[6, rows, ln], blk))
        yield
        wb = (w + wq).astype(BF16)
        af_sc[1, rows, ln] = xy[CH:]
        u_ = _bdot(wb, _block_diag(xy[:CH].astype(BF16), blk))
        wk = _bdot(wb, _block_diag(kt, blk))
        bk = jnp.concatenate([strm_sc[4, rows, ln], strm_sc[5, rows, ln]], axis=0).astype(F32)
        bkt_sc[c, gi] = bk.T.astype(BF16)
        gcol_sc[c, gi] = jnp.broadcast_to(gam_sc[c * CH:c * CH + 1, ln], (GW, GW)).T
        yield
        af_sc[0, rows, ln] = -u_
        ab_sc[0, rows, ln] = (-wk).astype(BF16)
        done.add((c, gi))

    def chain_b(gi):
        ln = slice(gi * GW, (gi + 1) * GW)
        st = s_sc[gi]
        for c in range(n_chk):
            while (c, gi) not in done:
                yield
            rows = slice(c * CH, (c + 1) * CH)
            zz = _bdot(jnp.concatenate([ab_sc[0, rows, ln], strm_sc[1, rows, ln]], axis=0), st)
            yield
            dmb = (af_sc[0, rows, ln] + zz[:CH]).astype(BF16)
            upd = jnp.dot(bkt_sc[c, gi], jnp.concatenate([dmb, strm_sc[6, rows, ln]], axis=0),
                          preferred_element_type=F32)
            yd = _bdot(ab_sc[1, rows, ln], _block_diag(dmb, blk))
            yield
            st = gcol_sc[c, gi] * st + upd * blk_f
            y_sc[rows, ln] = zz[CH:] + af_sc[1, rows, ln] + yd
        s_sc[gi] = st

    a_chains = [chain_a(c, gi) for c in range(n_chk) for gi in range(n_grp)]
    b_chains = [chain_b(gi) for gi in range(n_grp)]
    _interleave(a_chains, b_chains, A_WIDTH)

    yw = y_sc[...]
    mu_h = _seg_sum(yw, ones_bd) * (1.0 / HEAD)
    dy = yw - mu_h
    var_h = _seg_sum(dy * dy, ones_bd) * (1.0 / HEAD)
    yn = dy * lax.rsqrt(var_h + GN_EPS) * lg_ref[...] + lb_ref[...]
    y_rwkv = (yn + bonus) * g

    o_ref[0] = x + _bdot(y_conv, wout_ref[0:CONV_W, :]) + _bdot(y_rwkv, wout_ref[CONV_W:, :])


def _const_spec(shape):
    nd = len(shape)
    return pl.BlockSpec(shape, lambda *_: (0,) * nd)


def _mixer(x, gmix, win, cw, cb, clg, clb, mu, w0, wd, a0, wa, wg, kkw, ka, rk, lg, lb, wout, tri, ones_bd, blk):
    B, S, D = x.shape
    consts = (gmix, win, cw, cb, clg, clb, mu, w0, wd, a0, wa, wg, kkw, ka, rk, lg, lb, wout, tri, ones_bd, blk)
    return pl.pallas_call(
        _mixer_kernel,
        grid=(B, S // TT),
        in_specs=[pl.BlockSpec((1, TT, D), lambda b, j: (b, j, 0))] + [_const_spec(c.shape) for c in consts],
        out_specs=pl.BlockSpec((1, TT, D), lambda b, j: (b, j, 0)),
        out_shape=jax.ShapeDtypeStruct((B, S, D), F32),
        scratch_shapes=[
            pltpu.VMEM((8 + TT, P_COLS), F32),
            pltpu.VMEM((HALO + TT, CONV_W), F32),
            pltpu.VMEM((TT, CONV_W), F32),
            pltpu.VMEM((7, TT, RW), BF16),
            pltpu.VMEM((2, TT, RW), F32),
            pltpu.VMEM((2, TT, RW), BF16),
            pltpu.VMEM((TT // CH, RW // GW, GW, 2 * CH), BF16),
            pltpu.VMEM((TT // CH, RW // GW, GW, GW), F32),
            pltpu.VMEM((TT, RW), F32),
            pltpu.VMEM((TT, RW), F32),
            pltpu.VMEM((RW // GW, GW, GW), F32),
        ],
        compiler_params=pltpu.CompilerParams(
            dimension_semantics=("arbitrary", "arbitrary"), vmem_limit_bytes=VMEM_LIMIT),
        name="mixer",
    )(x, *consts)


def _xattn_kernel(x_ref, mem_ref, gc_ref, gm_ref, wq_ref, wk_ref, wv_ref, wo_ref, o_ref, k_sc, v_sc):
    @pl.when(pl.program_id(1) == 0)
    def _():
        mn = _rms(mem_ref[0], gm_ref[...]).astype(BF16)
        k_sc[...] = jnp.dot(mn, wk_ref[...], preferred_element_type=F32).astype(BF16)
        v_sc[...] = jnp.dot(mn, wv_ref[...], preferred_element_type=F32).astype(BF16)

    x = x_ref[0]
    q = _bdot(_rms(x, gc_ref[...]), wq_ref[...])
    outs = []
    for hd in range(N_XH):
        sl = slice(hd * XHD, (hd + 1) * XHD)
        s = _bdot_nt(q[:, sl], k_sc[:, sl]) * (XHD ** -0.5)
        s = s - jnp.max(s, axis=-1, keepdims=True)
        e = jnp.exp(s)
        p = e / jnp.sum(e, axis=-1, keepdims=True)
        outs.append(_bdot(p, v_sc[:, sl]))
    o_ref[0] = x + _bdot(jnp.concatenate(outs, axis=1), wo_ref[...])


def _xattn(x, mem, gc, gm, wq, wk, wv, wo):
    B, S, D = x.shape
    M = mem.shape[1]
    consts = (gc, gm, wq, wk, wv, wo)
    return pl.pallas_call(
        _xattn_kernel,
        grid=(B, S // TQ),
        in_specs=[pl.BlockSpec((1, TQ, D), lambda b, j: (b, j, 0)),
                  pl.BlockSpec((1, M, D), lambda b, j: (b, 0, 0))] + [_const_spec(c.shape) for c in consts],
        out_specs=pl.BlockSpec((1, TQ, D), lambda b, j: (b, j, 0)),
        out_shape=jax.ShapeDtypeStruct((B, S, D), F32),
        scratch_shapes=[pltpu.VMEM((M, D), BF16), pltpu.VMEM((M, D), BF16)],
        compiler_params=pltpu.CompilerParams(
            dimension_semantics=("arbitrary", "arbitrary"), vmem_limit_bytes=VMEM_LIMIT),
        name="xattn",
    )(x, mem, *consts)


def _ffn_kernel(x_ref, gf_ref, w1_ref, w2_ref, gfin_ref, o_ref, *, final_norm):
    x = x_ref[...]
    h = _rms(x, gf_ref[...]).astype(BF16)
    acc = x
    for c in range(D_FF // FF_CH):
        sl = slice(c * FF_CH, (c + 1) * FF_CH)
        t = jnp.maximum(jnp.dot(h, w1_ref[:, sl], preferred_element_type=F32), 0.0)
        acc = acc + _bdot(t * t, w2_ref[sl, :])
    if final_norm:
        acc = _rms(acc, gfin_ref[...])
    o_ref[...] = acc


def _ffn(x2d, gf, w1, w2, gfin, final_norm):
    M, D = x2d.shape
    consts = (gf, w1, w2, gfin)
    return pl.pallas_call(
        functools.partial(_ffn_kernel, final_norm=final_norm),
        grid=(M // TM,),
        in_specs=[pl.BlockSpec((TM, D), lambda i: (i, 0))] + [_const_spec(c.shape) for c in consts],
        out_specs=pl.BlockSpec((TM, D), lambda i: (i, 0)),
        out_shape=jax.ShapeDtypeStruct((M, D), F32),
        compiler_params=pltpu.CompilerParams(
            dimension_semantics=("arbitrary",), vmem_limit_bytes=VMEM_LIMIT),
        name="ffn",
    )(x2d, *consts)


def _row(v):
    return v.reshape(1, -1).astype(F32)


def _mixer_constants():
    i = jnp.arange(TT)
    tri = ((i[:, None] // CH == i[None, :] // CH) & (i[:, None] >= i[None, :])).astype(BF16)
    jj = jnp.arange(GW)
    blk = (jj[:, None] // HEAD == jj[None, :] // HEAD).astype(BF16)
    return tri, blk, blk


def kernel(x, mem, g_mix, w_in, conv_w, conv_b, conv_ln_g, conv_ln_b, mu_b, w0, w_decay2, a0, a_lora2,
           g_lora2, k_k, k_a, r_k, lnx_g, lnx_b, w_out, g_cross, g_mem, wq, wk, wv, wo, g_ffn, w_ff1, w_ff2,
           g_final):
    B, S, D = x.shape
    depth = w_in.shape[0]
    tri, ones_bd, blk = _mixer_constants()
    pad_cols = LORA_PAD - (DEC_L + AAA_L + GATE_L)
    for l in range(depth):
        win = jnp.pad(w_in[l], ((0, 0), (0, pad_cols))).astype(BF16)
        mu = _row(jnp.pad(mu_b[l], (0, pad_cols)))
        wd = jnp.zeros((LORA_PAD, RW), F32).at[0:DEC_L].set(w_decay2[l]).astype(BF16)
        wa = jnp.zeros((LORA_PAD, RW), F32).at[DEC_L:DEC_L + AAA_L].set(a_lora2[l]).astype(BF16)
        wg = jnp.zeros((LORA_PAD, RW), F32).at[DEC_L + AAA_L:DEC_L + AAA_L + GATE_L].set(g_lora2[l]).astype(BF16)
        x = _mixer(x, _row(g_mix[l]), win, conv_w[l].astype(F32), _row(conv_b[l]), _row(conv_ln_g[l]),
                   _row(conv_ln_b[l]), mu, _row(w0[l]), wd, _row(a0[l]), wa, wg, _row(k_k[l]), _row(k_a[l]),
                   _row(r_k[l]), _row(lnx_g[l]), _row(lnx_b[l]), w_out[l].astype(BF16), tri, ones_bd, blk)
        x = _xattn(x, mem, _row(g_cross[l]), _row(g_mem[l]), wq[l].astype(BF16), wk[l].astype(BF16),
                   wv[l].astype(BF16), wo[l].astype(BF16))
        x = _ffn(x.reshape(B * S, D), _row(g_ffn[l]), w_ff1[l].astype(BF16), w_ff2[l].astype(BF16),
                 _row(g_final), final_norm=(l == depth - 1)).reshape(B, S, D)
    return x
```

```python
import functools

import jax
import jax.numpy as jnp
from jax import lax
from jax.experimental import pallas as pl
from jax.experimental.pallas import tpu as pltpu

F32 = jnp.float32
BF16 = jnp.bfloat16

D_MODEL = 1024
CONV_W = 512
CONV_K = 31
RW = 512
HEAD = 64
LORA_PAD = 256
DEC_L, AAA_L, GATE_L = 32, 32, 96
P_COLS = 2 * CONV_W + 3 * RW + LORA_PAD
RWKV_P = 3 * RW + LORA_PAD
N_XH = 4
XHD = D_MODEL // N_XH
MEM_LEN = 256
D_FF = 4 * D_MODEL
RMS_EPS = 1e-6
LN_EPS = 1e-5
GN_EPS = 1e-5 * HEAD

CH = 64
GRP = 4
GW = GRP * HEAD
TT = 256
NB = 2
A_WIDTH = 8
HALO = 32
CONV_RB = 32
TQ = 512
TM = 512
FF_CH = 1024
VMEM_LIMIT = 56 * 1024 * 1024


def _bdot(a, b):
    return jnp.dot(a.astype(BF16), b.astype(BF16), preferred_element_type=F32)


def _bdot_nt(a, b):
    return lax.dot_general(a.astype(BF16), b.astype(BF16), (((1,), (1,)), ((), ())),
                           preferred_element_type=F32)


def _rms(x, g):
    return x * lax.rsqrt(jnp.mean(x * x, axis=-1, keepdims=True) + RMS_EPS) * g


def _seg_sum(x, ones_bd):
    hi = x.astype(BF16)
    lo = (x - hi.astype(F32)).astype(BF16)
    outs = []
    for g in range(RW // GW):
        sl = slice(g * GW, (g + 1) * GW)
        outs.append(jnp.dot(hi[:, sl], ones_bd, preferred_element_type=F32)
                    + jnp.dot(lo[:, sl], ones_bd, preferred_element_type=F32))
    return jnp.concatenate(outs, axis=1)


def _block_diag(z, blk):
    return jnp.concatenate([z] * GRP, axis=0) * blk


def _interleave(a_chains, b_chains, width):
    pending = list(a_chains)
    live_a, live_b = [], list(b_chains)
    while pending or live_a or live_b:
        while pending and len(live_a) < width:
            live_a.append(pending.pop(0))
        nxt_a, nxt_b = [], []
        for group, nxt in ((live_a, nxt_a), (live_b, nxt_b)):
            for ch in group:
                try:
                    next(ch)
                    nxt.append(ch)
                except StopIteration:
                    pass
        live_a, live_b = nxt_a, nxt_b


def _mixer_kernel(x_ref, gmix_ref, win_ref, cw_ref, cb_ref, clg_ref, clb_ref, mu_ref, w0_ref, wd_ref,
                  a0_ref, wa_ref, wg_ref, kkw_ref, ka_ref, rk_ref, lg_ref, lb_ref, wout_ref,
                  tri_ref, ones_ref, blk_ref,
                  o_ref,
                  p_sc, prev_sc, u_sc, ush_sc, yc_sc, g_sc, bonus_sc, strm_sc, af_sc, ab_sc, bkt_sc, gcol_sc, gam_sc, y_sc,
                  s_sc):
    j = pl.program_id(1)

    @pl.when(j == 0)
    def _():
        prev_sc[...] = jnp.zeros(prev_sc.shape, F32)
        u_sc[:, 0:HALO, :] = jnp.zeros((NB, HALO, CONV_W), F32)
        s_sc[...] = jnp.zeros(s_sc.shape, F32)

    ones_bd = ones_ref[...]
    blk = blk_ref[...]
    blk_f = blk.astype(F32)
    ri = lax.broadcasted_iota(jnp.int32, (CH, GW), 0)
    ci = lax.broadcasted_iota(jnp.int32, (CH, GW), 1) % CH
    tri_s = ri > ci
    tri_i = ri >= ci
    eye_cat = (ri == ci).astype(F32)
    n_grp = RW // GW
    n_chk = TT // CH
    done = set()

    def stage_inproj(bi):
        h = _rms(x_ref[bi], gmix_ref[...])
        p_sc[bi] = _bdot(h, win_ref[...])

    def stage_conv(bi):
        u_sc[bi, HALO:HALO + TT, :] = p_sc[bi, :, 0:CONV_W] * jax.nn.sigmoid(p_sc[bi, :, CONV_W:2 * CONV_W])
        ua = u_sc[bi]
        for sh in range(1, 8):
            ush_sc[bi, sh - 1] = pltpu.roll(ua, HALO + TT - sh, axis=0)
        off = HALO - (CONV_K - 1)
        for rb in range(TT // CONV_RB):
            base = rb * CONV_RB
            acc = jnp.broadcast_to(cb_ref[...], (CONV_RB, CONV_W))
            for t in range(CONV_K):
                m8, sh = divmod(off + t, 8)
                lo = base + 8 * m8
                src = u_sc[bi, lo:lo + CONV_RB, :] if sh == 0 else ush_sc[bi, sh - 1, lo:lo + CONV_RB, :]
                acc = acc + cw_ref[t:t + 1, :] * src
            m = jnp.mean(acc, axis=-1, keepdims=True)
            dlt = acc - m
            var = jnp.mean(dlt * dlt, axis=-1, keepdims=True)
            y_conv = jax.nn.silu(dlt * lax.rsqrt(var + LN_EPS) * clg_ref[...] + clb_ref[...])
            yc_sc[bi, base:base + CONV_RB, :] = y_conv.astype(BF16)
        u_sc[bi, 0:HALO, :] = u_sc[bi, TT:TT + HALO, :]

    def stage_prep(bi):
        pb = p_sc[bi, :, 2 * CONV_W:]
        rolled = pltpu.roll(pb, 1, axis=0)
        first = lax.broadcasted_iota(jnp.int32, (8, RWKV_P), 0) == 0
        prev = jnp.concatenate([jnp.where(first, prev_sc[bi, 0:1, :], rolled[0:8]), rolled[8:]], axis=0)
        prev_sc[bi, 0:1, :] = pb[TT - 1:TT, :]
        z = pb + mu_ref[...] * (prev - pb)
        r = z[:, 0:RW]
        k = z[:, RW:2 * RW]
        v = z[:, 2 * RW:3 * RW]
        zl = z[:, 3 * RW:]
        dec = _bdot(jnp.tanh(zl), wd_ref[...])
        w_log = -jax.nn.softplus(-(w0_ref[...] + dec)) - 0.5
        lw = -jnp.exp(w_log)
        a = jax.nn.sigmoid(a0_ref[...] + _bdot(zl, wa_ref[...]))
        g_sc[bi] = _bdot(jax.nn.sigmoid(zl), wg_ref[...])
        kk = k * kkw_ref[...]
        kk = kk / jnp.maximum(jnp.sqrt(_seg_sum(kk * kk, ones_bd)), 1e-12)
        k = k * (1.0 + (a - 1.0) * ka_ref[...])
        bonus_sc[bi] = _seg_sum(r * k * rk_ref[...], ones_bd) * v
        l1 = lw.astype(BF16)
        l2 = (lw - l1.astype(F32)).astype(BF16)
        l3 = (lw - l1.astype(F32) - l2.astype(F32)).astype(BF16)
        tri = tri_ref[...]
        cs = (jnp.dot(tri, l1, preferred_element_type=F32) + jnp.dot(tri, l2, preferred_element_type=F32)
              + jnp.dot(tri, l3, preferred_element_type=F32))
        cl = jnp.concatenate(
            [jnp.broadcast_to(cs[c * CH + CH - 1:c * CH + CH, :], (CH, RW)) for c in range(n_chk)], axis=0)
        e_in = jnp.exp(cs)
        e_inv = jnp.exp(-cs)
        e_end = jnp.exp(cl - cs)
        kka = kk * a
        strm_sc[bi, 0] = (kk * jnp.exp(cs - lw)).astype(BF16)
        strm_sc[bi, 1] = (r * e_in).astype(BF16)
        strm_sc[bi, 2] = (kka * e_inv).astype(BF16)
        strm_sc[bi, 3] = (k * e_inv).astype(BF16)
        strm_sc[bi, 4] = (kka * e_end).astype(BF16)
        strm_sc[bi, 5] = (k * e_end).astype(BF16)
        strm_sc[bi, 6] = v.astype(BF16)
        gam_sc[bi] = jnp.exp(cl)

    def chain_a(bi, c, gi):
        rows = slice(c * CH, (c + 1) * CH)
        ln = slice(gi * GW, (gi + 1) * GW)
        kt = strm_sc[bi, 0, rows, ln]
        lhs = jnp.concatenate([kt, strm_sc[bi, 1, rows, ln]], axis=0)
        gb = _bdot_nt(lhs, _block_diag(strm_sc[bi, 2, rows, ln], blk))
        gk = _bdot_nt(lhs, _block_diag(strm_sc[bi, 3, rows, ln], blk))
        yield
        pm = jnp.where(tri_s, -gb[:CH], 0.0)
        lk = jnp.where(tri_s, gk[:CH], 0.0)
        ab_sc[bi, 1, rows, ln] = jnp.where(tri_i, gb[CH:], 0.0).astype(BF16)
        ark = jnp.where(tri_i, gk[CH:], 0.0)
        w = eye_cat + pm
        q = _bdot(pm, _block_diag(pm.astype(BF16), blk))
        yield
        for _ in range(4):
            rr = _bdot(jnp.concatenate([w, q], axis=0), _block_diag(q.astype(BF16), blk))
            yield
            w = w + rr[:CH]
            q = rr[CH:]
        wq = _bdot(w, _block_diag(q.astype(BF16), blk))
        xy = _bdot(jnp.concatenate([lk, ark], axis=0), _block_diag(strm_sc[bi, 6, rows, ln], blk))
        yield
        wb = (w + wq).astype(BF16)
        af_sc[bi, 1, rows, ln] = xy[CH:]
        u_ = _bdot(wb, _block_diag(xy[:CH].astype(BF16), blk))
        wk = _bdot(wb, _block_diag(kt, blk))
        bk = jnp.concatenate([strm_sc[bi, 4, rows, ln], strm_sc[bi, 5, rows, ln]], axis=0).astype(F32)
        bkt_sc[bi, c, gi] = bk.T.astype(BF16)
        gcol_sc[bi, c, gi] = jnp.broadcast_to(gam_sc[bi, c * CH:c * CH + 1, ln], (GW, GW)).T
        yield
        af_sc[bi, 0, rows, ln] = -u_
        ab_sc[bi, 0, rows, ln] = (-wk).astype(BF16)
        done.add((bi, c, gi))

    def chain_b(bi, gi):
        ln = slice(gi * GW, (gi + 1) * GW)
        st = s_sc[bi, gi]
        for c in range(n_chk):
            while (bi, c, gi) not in done:
                yield
            rows = slice(c * CH, (c + 1) * CH)
            zz = _bdot(jnp.concatenate([ab_sc[bi, 0, rows, ln], strm_sc[bi, 1, rows, ln]], axis=0), st)
            yield
            dmb = (af_sc[bi, 0, rows, ln] + zz[:CH]).astype(BF16)
            upd = jnp.dot(bkt_sc[bi, c, gi], jnp.concatenate([dmb, strm_sc[bi, 6, rows, ln]], axis=0),
                          preferred_element_type=F32)
            yd = _bdot(ab_sc[bi, 1, rows, ln], _block_diag(dmb, blk))
            yield
            st = gcol_sc[bi, c, gi] * st + upd * blk_f
            y_sc[bi, rows, ln] = zz[CH:] + af_sc[bi, 1, rows, ln] + yd
        s_sc[bi, gi] = st

    def stage_tail(bi):
        yw = y_sc[bi]
        mu_h = _seg_sum(yw, ones_bd) * (1.0 / HEAD)
        dy = yw - mu_h
        var_h = _seg_sum(dy * dy, ones_bd) * (1.0 / HEAD)
        yn = dy * lax.rsqrt(var_h + GN_EPS) * lg_ref[...] + lb_ref[...]
        y_rwkv = (yn + bonus_sc[bi]) * g_sc[bi]
        o_ref[bi] = (x_ref[bi] + jnp.dot(yc_sc[bi], wout_ref[0:CONV_W, :], preferred_element_type=F32)
                     + _bdot(y_rwkv, wout_ref[CONV_W:, :]))

    def stage_wkv(bi):
        a_chains = [chain_a(bi, c, gi) for c in range(n_chk) for gi in range(n_grp)]
        b_chains = [chain_b(bi, gi) for gi in range(n_grp)]
        _interleave(a_chains, b_chains, A_WIDTH)

    for bi in range(NB):
        stage_inproj(bi)
    for bi in range(NB):
        stage_conv(bi)
        stage_prep(bi)
        if bi > 0:
            stage_tail(bi - 1)
        stage_wkv(bi)
    stage_tail(NB - 1)


def _const_spec(shape):
    nd = len(shape)
    return pl.BlockSpec(shape, lambda *_: (0,) * nd, pipeline_mode=pl.Buffered(1))


def _mixer(x, gmix, win, cw, cb, clg, clb, mu, w0, wd, a0, wa, wg, kkw, ka, rk, lg, lb, wout, tri, ones_bd, blk):
    B, S, D = x.shape
    consts = (gmix, win, cw, cb, clg, clb, mu, w0, wd, a0, wa, wg, kkw, ka, rk, lg, lb, wout, tri, ones_bd, blk)
    return pl.pallas_call(
        _mixer_kernel,
        grid=(B // NB, S // TT),
        in_specs=[pl.BlockSpec((NB, TT, D), lambda b, j: (b, j, 0))] + [_const_spec(c.shape) for c in consts],
        out_specs=pl.BlockSpec((NB, TT, D), lambda b, j: (b, j, 0)),
        out_shape=jax.ShapeDtypeStruct((B, S, D), F32),
        scratch_shapes=[
            pltpu.VMEM((NB, TT, P_COLS), F32),
            pltpu.VMEM((NB, 8, RWKV_P), F32),
            pltpu.VMEM((NB, HALO + TT, CONV_W), F32),
            pltpu.VMEM((NB, 7, HALO + TT, CONV_W), F32),
            pltpu.VMEM((NB, TT, CONV_W), BF16),
            pltpu.VMEM((NB, TT, RW), F32),
            pltpu.VMEM((NB, TT, RW), F32),
            pltpu.VMEM((NB, 7, TT, RW), BF16),
            pltpu.VMEM((NB, 2, TT, RW), F32),
            pltpu.VMEM((NB, 2, TT, RW), BF16),
            pltpu.VMEM((NB, TT // CH, RW // GW, GW, 2 * CH), BF16),
            pltpu.VMEM((NB, TT // CH, RW // GW, GW, GW), F32),
            pltpu.VMEM((NB, TT, RW), F32),
            pltpu.VMEM((NB, TT, RW), F32),
            pltpu.VMEM((NB, RW // GW, GW, GW), F32),
        ],
        compiler_params=pltpu.CompilerParams(
            dimension_semantics=("arbitrary", "arbitrary"), vmem_limit_bytes=VMEM_LIMIT),
        name="mixer",
    )(x, *consts)


def _xattn_kernel(x_ref, mem_ref, gc_ref, gm_ref, wq_ref, wk_ref, wv_ref, wo_ref, o_ref, k_sc, v_sc):
    @pl.when(pl.program_id(1) == 0)
    def _():
        mn = _rms(mem_ref[0], gm_ref[...]).astype(BF16)
        k_sc[...] = jnp.dot(mn, wk_ref[...], preferred_element_type=F32).astype(BF16)
        v_sc[...] = jnp.dot(mn, wv_ref[...], preferred_element_type=F32).astype(BF16)

    x = x_ref[0]
    q = _bdot(_rms(x, gc_ref[...]), wq_ref[...])
    outs = []
    for hd in range(N_XH):
        sl = slice(hd * XHD, (hd + 1) * XHD)
        s = _bdot_nt(q[:, sl], k_sc[:, sl]) * (XHD ** -0.5)
        s = s - jnp.max(s, axis=-1, keepdims=True)
        e = jnp.exp(s)
        p = e / jnp.sum(e, axis=-1, keepdims=True)
        outs.append(_bdot(p, v_sc[:, sl]))
    o_ref[0] = x + _bdot(jnp.concatenate(outs, axis=1), wo_ref[...])


def _xattn(x, mem, gc, gm, wq, wk, wv, wo):
    B, S, D = x.shape
    M = mem.shape[1]
    consts = (gc, gm, wq, wk, wv, wo)
    return pl.pallas_call(
        _xattn_kernel,
        grid=(B, S // TQ),
        in_specs=[pl.BlockSpec((1, TQ, D), lambda b, j: (b, j, 0)),
                  pl.BlockSpec((1, M, D), lambda b, j: (b, 0, 0))] + [_const_spec(c.shape) for c in consts],
        out_specs=pl.BlockSpec((1, TQ, D), lambda b, j: (b, j, 0)),
        out_shape=jax.ShapeDtypeStruct((B, S, D), F32),
        scratch_shapes=[pltpu.VMEM((M, D), BF16), pltpu.VMEM((M, D), BF16)],
        compiler_params=pltpu.CompilerParams(
            dimension_semantics=("arbitrary", "arbitrary"), vmem_limit_bytes=VMEM_LIMIT),
        name="xattn",
    )(x, mem, *consts)


def _ffn_kernel(x_ref, gf_ref, w1_ref, w2_ref, gfin_ref, o_ref, *, final_norm):
    x = x_ref[...]
    h = _rms(x, gf_ref[...]).astype(BF16)
    acc = x
    for c in range(D_FF // FF_CH):
        sl = slice(c * FF_CH, (c + 1) * FF_CH)
        t = jnp.maximum(jnp.dot(h, w1_ref[:, sl], preferred_element_type=F32), 0.0)
        acc = acc + _bdot(t * t, w2_ref[sl, :])
    if final_norm:
        acc = _rms(acc, gfin_ref[...])
    o_ref[...] = acc


def _ffn(x2d, gf, w1, w2, gfin, final_norm):
    M, D = x2d.shape
    consts = (gf, w1, w2, gfin)
    return pl.pallas_call(
        functools.partial(_ffn_kernel, final_norm=final_norm),
        grid=(M // TM,),
        in_specs=[pl.BlockSpec((TM, D), lambda i: (i, 0))] + [_const_spec(c.shape) for c in consts],
        out_specs=pl.BlockSpec((TM, D), lambda i: (i, 0)),
        out_shape=jax.ShapeDtypeStruct((M, D), F32),
        compiler_params=pltpu.CompilerParams(
            dimension_semantics=("arbitrary",), vmem_limit_bytes=VMEM_LIMIT),
        name="ffn",
    )(x2d, *consts)


def _row(v):
    return v.reshape(1, -1).astype(F32)


def _mixer_constants():
    i = jnp.arange(TT)
    tri = ((i[:, None] // CH == i[None, :] // CH) & (i[:, None] >= i[None, :])).astype(BF16)
    jj = jnp.arange(GW)
    blk = (jj[:, None] // HEAD == jj[None, :] // HEAD).astype(BF16)
    return tri, blk, blk


def kernel(x, mem, g_mix, w_in, conv_w, conv_b, conv_ln_g, conv_ln_b, mu_b, w0, w_decay2, a0, a_lora2,
           g_lora2, k_k, k_a, r_k, lnx_g, lnx_b, w_out, g_cross, g_mem, wq, wk, wv, wo, g_ffn, w_ff1, w_ff2,
           g_final):
    B, S, D = x.shape
    depth = w_in.shape[0]
    tri, ones_bd, blk = _mixer_constants()
    pad_cols = LORA_PAD - (DEC_L + AAA_L + GATE_L)
    for l in range(depth):
        win = jnp.pad(w_in[l], ((0, 0), (0, pad_cols))).astype(BF16)
        mu = _row(jnp.pad(mu_b[l], (0, pad_cols)))
        wd = jnp.zeros((LORA_PAD, RW), F32).at[0:DEC_L].set(w_decay2[l]).astype(BF16)
        wa = jnp.zeros((LORA_PAD, RW), F32).at[DEC_L:DEC_L + AAA_L].set(a_lora2[l]).astype(BF16)
        wg = jnp.zeros((LORA_PAD, RW), F32).at[DEC_L + AAA_L:DEC_L + AAA_L + GATE_L].set(g_lora2[l]).astype(BF16)
        x = _mixer(x, _row(g_mix[l]), win, conv_w[l].astype(F32), _row(conv_b[l]), _row(conv_ln_g[l]),
                   _row(conv_ln_b[l]), mu, _row(w0[l]), wd, _row(a0[l]), wa, wg, _row(k_k[l]), _row(k_a[l]),
                   _row(r_k[l]), _row(lnx_g[l]), _row(lnx_b[l]), w_out[l].astype(BF16), tri, ones_bd, blk)
        x = _xattn(x, mem, _row(g_cross[l]), _row(g_mem[l]), wq[l].astype(BF16), wk[l].astype(BF16),
                   wv[l].astype(BF16), wo[l].astype(BF16))
        x = _ffn(x.reshape(B * S, D), _row(g_ffn[l]), w_ff1[l].astype(BF16), w_ff2[l].astype(BF16),
                 _row(g_final), final_norm=(l == depth - 1)).reshape(B, S, D)
    return x
```

```python
import functools

import jax
import jax.numpy as jnp
from jax import lax
from jax.experimental import pallas as pl
from jax.experimental.pallas import tpu as pltpu

F32 = jnp.float32
BF16 = jnp.bfloat16

D_MODEL = 1024
CONV_W = 512
CONV_K = 31
RW = 512
HEAD = 64
LORA_PAD = 256
DEC_L, AAA_L, GATE_L = 32, 32, 96
P_COLS = 2 * CONV_W + 3 * RW + LORA_PAD
RWKV_P = 3 * RW + LORA_PAD
N_XH = 4
XHD = D_MODEL // N_XH
MEM_LEN = 256
D_FF = 4 * D_MODEL
RMS_EPS = 1e-6
LN_EPS = 1e-5
GN_EPS = 1e-5 * HEAD

CH = 64
GRP = 4
GW = GRP * HEAD
TT = 256
NB = 2
A_WIDTH = 8
HALO = 32
CONV_RB = 32
TQ = 512
TM = 512
FF_CH = 1024
VMEM_LIMIT = 56 * 1024 * 1024


def _bdot(a, b):
    return jnp.dot(a.astype(BF16), b.astype(BF16), preferred_element_type=F32)


def _bdot_nt(a, b):
    return lax.dot_general(a.astype(BF16), b.astype(BF16), (((1,), (1,)), ((), ())),
                           preferred_element_type=F32)


def _rms(x, g):
    return x * lax.rsqrt(jnp.mean(x * x, axis=-1, keepdims=True) + RMS_EPS) * g


def _seg_sum(x, ones_bd):
    hi = x.astype(BF16)
    outs = []
    for g in range(RW // GW):
        sl = slice(g * GW, (g + 1) * GW)
        outs.append(jnp.dot(hi[:, sl], ones_bd, preferred_element_type=F32))
    return jnp.concatenate(outs, axis=1)


def _block_diag(z, blk):
    return jnp.concatenate([z] * GRP, axis=0) * blk


def _interleave(a_chains, b_chains, width):
    pending = list(a_chains)
    live_a, live_b = [], list(b_chains)
    while pending or live_a or live_b:
        while pending and len(live_a) < width:
            live_a.append(pending.pop(0))
        nxt_a, nxt_b = [], []
        for group, nxt in ((live_a, nxt_a), (live_b, nxt_b)):
            for ch in group:
                try:
                    next(ch)
                    nxt.append(ch)
                except StopIteration:
                    pass
        live_a, live_b = nxt_a, nxt_b


def _mixer_kernel(x_ref, gmix_ref, win_ref, cw_ref, cb_ref, clg_ref, clb_ref, mu_ref, w0_ref, wd_ref,
                  a0_ref, wa_ref, wg_ref, kkw_ref, ka_ref, rk_ref, lg_ref, lb_ref, wout_ref,
                  tri_ref, ones_ref, blk_ref,
                  o_ref,
                  p_sc, prev_sc, u_sc, ush_sc, yc_sc, g_sc, bonus_sc, strm_sc, af_sc, ab_sc, bkt_sc, gcol_sc, gam_sc, y_sc,
                  s_sc):
    j = pl.program_id(1)

    @pl.when(j == 0)
    def _():
        prev_sc[...] = jnp.zeros(prev_sc.shape, F32)
        u_sc[:, 0:HALO, :] = jnp.zeros((NB, HALO, CONV_W), F32)
        s_sc[...] = jnp.zeros(s_sc.shape, F32)

    ones_bd = ones_ref[...]
    blk = blk_ref[...]
    blk_f = blk.astype(F32)
    ri = lax.broadcasted_iota(jnp.int32, (CH, GW), 0)
    ci = lax.broadcasted_iota(jnp.int32, (CH, GW), 1) % CH
    tri_s = ri > ci
    tri_i = ri >= ci
    eye_cat = (ri == ci).astype(F32)
    n_grp = RW // GW
    n_chk = TT // CH
    done = set()

    def stage_inproj(bi):
        h = _rms(x_ref[bi], gmix_ref[...])
        p_sc[bi] = _bdot(h, win_ref[...])

    def stage_conv(bi):
        u_sc[bi, HALO:HALO + TT, :] = p_sc[bi, :, 0:CONV_W] * jax.nn.sigmoid(p_sc[bi, :, CONV_W:2 * CONV_W])
        ua = u_sc[bi]
        for sh in range(1, 8):
            ush_sc[bi, sh - 1] = pltpu.roll(ua, HALO + TT - sh, axis=0)
        off = HALO - (CONV_K - 1)
        for rb in range(TT // CONV_RB):
            base = rb * CONV_RB
            acc = jnp.broadcast_to(cb_ref[...], (CONV_RB, CONV_W))
            for t in range(CONV_K):
                m8, sh = divmod(off + t, 8)
                lo = base + 8 * m8
                src = u_sc[bi, lo:lo + CONV_RB, :] if sh == 0 else ush_sc[bi, sh - 1, lo:lo + CONV_RB, :]
                acc = acc + cw_ref[t:t + 1, :] * src
            m = jnp.mean(acc, axis=-1, keepdims=True)
            dlt = acc - m
            var = jnp.mean(dlt * dlt, axis=-1, keepdims=True)
            y_conv = jax.nn.silu(dlt * lax.rsqrt(var + LN_EPS) * clg_ref[...] + clb_ref[...])
            yc_sc[bi, base:base + CONV_RB, :] = y_conv.astype(BF16)
        u_sc[bi, 0:HALO, :] = u_sc[bi, TT:TT + HALO, :]

    def stage_prep(bi):
        pb = p_sc[bi, :, 2 * CONV_W:]
        rolled = pltpu.roll(pb, 1, axis=0)
        first = lax.broadcasted_iota(jnp.int32, (8, RWKV_P), 0) == 0
        prev = jnp.concatenate([jnp.where(first, prev_sc[bi, 0:1, :], rolled[0:8]), rolled[8:]], axis=0)
        prev_sc[bi, 0:1, :] = pb[TT - 1:TT, :]
        z = pb + mu_ref[...] * (prev - pb)
        r = z[:, 0:RW]
        k = z[:, RW:2 * RW]
        v = z[:, 2 * RW:3 * RW]
        zl = z[:, 3 * RW:]
        dec = _bdot(jnp.tanh(zl), wd_ref[...])
        w_log = -jax.nn.softplus(-(w0_ref[...] + dec)) - 0.5
        lw = -jnp.exp(w_log)
        a = jax.nn.sigmoid(a0_ref[...] + _bdot(zl, wa_ref[...]))
        g_sc[bi] = _bdot(jax.nn.sigmoid(zl), wg_ref[...])
        kk = k * kkw_ref[...]
        kk = kk / jnp.maximum(jnp.sqrt(_seg_sum(kk * kk, ones_bd)), 1e-12)
        k = k * (1.0 + (a - 1.0) * ka_ref[...])
        bonus_sc[bi] = _seg_sum(r * k * rk_ref[...], ones_bd) * v
        l1 = lw.astype(BF16)
        l2 = (lw - l1.astype(F32)).astype(BF16)
        l3 = (lw - l1.astype(F32) - l2.astype(F32)).astype(BF16)
        tri = tri_ref[...]
        cs = (jnp.dot(tri, l1, preferred_element_type=F32) + jnp.dot(tri, l2, preferred_element_type=F32)
              + jnp.dot(tri, l3, preferred_element_type=F32))
        cl = jnp.concatenate(
            [jnp.broadcast_to(cs[c * CH + CH - 1:c * CH + CH, :], (CH, RW)) for c in range(n_chk)], axis=0)
        e_in = jnp.exp(cs)
        e_inv = jnp.exp(-cs)
        e_end = jnp.exp(cl - cs)
        kka = kk * a
        strm_sc[bi, 0] = (kk * jnp.exp(cs - lw)).astype(BF16)
        strm_sc[bi, 1] = (r * e_in).astype(BF16)
        strm_sc[bi, 2] = (kka * e_inv).astype(BF16)
        strm_sc[bi, 3] = (k * e_inv).astype(BF16)
        strm_sc[bi, 4] = (kka * e_end).astype(BF16)
        strm_sc[bi, 5] = (k * e_end).astype(BF16)
        strm_sc[bi, 6] = v.astype(BF16)
        gam_sc[bi] = jnp.exp(cl)

    def chain_a(bi, c, gi):
        rows = slice(c * CH, (c + 1) * CH)
        ln = slice(gi * GW, (gi + 1) * GW)
        kt = strm_sc[bi, 0, rows, ln]
        lhs = jnp.concatenate([kt, strm_sc[bi, 1, rows, ln]], axis=0)
        gb = _bdot_nt(lhs, _block_diag(strm_sc[bi, 2, rows, ln], blk))
        gk = _bdot_nt(lhs, _block_diag(strm_sc[bi, 3, rows, ln], blk))
        yield
        pm = jnp.where(tri_s, -gb[:CH], 0.0)
        lk = jnp.where(tri_s, gk[:CH], 0.0)
        ab_sc[bi, 1, rows, ln] = jnp.where(tri_i, gb[CH:], 0.0).astype(BF16)
        ark = jnp.where(tri_i, gk[CH:], 0.0)
        w = eye_cat + pm
        q = _bdot(pm, _block_diag(pm.astype(BF16), blk))
        yield
        for _ in range(4):
            rr = _bdot(jnp.concatenate([w, q], axis=0), _block_diag(q.astype(BF16), blk))
            yield
            w = w + rr[:CH]
            q = rr[CH:]
        wq = _bdot(w, _block_diag(q.astype(BF16), blk))
        xy = _bdot(jnp.concatenate([lk, ark], axis=0), _block_diag(strm_sc[bi, 6, rows, ln], blk))
        yield
        wb = (w + wq).astype(BF16)
        af_sc[bi, 1, rows, ln] = xy[CH:]
        u_ = _bdot(wb, _block_diag(xy[:CH].astype(BF16), blk))
        wk = _bdot(wb, _block_diag(kt, blk))
        bk = jnp.concatenate([strm_sc[bi, 4, rows, ln], strm_sc[bi, 5, rows, ln]], axis=0).astype(F32)
        bkt_sc[bi, c, gi] = bk.T.astype(BF16)
        gcol_sc[bi, c, gi] = jnp.broadcast_to(gam_sc[bi, c * CH:c * CH + 1, ln], (GW, GW)).T
        yield
        af_sc[bi, 0, rows, ln] = -u_
        ab_sc[bi, 0, rows, ln] = (-wk).astype(BF16)
        done.add((bi, c, gi))

    def chain_b(bi, gi):
        ln = slice(gi * GW, (gi + 1) * GW)
        st = s_sc[bi, gi]
        for c in range(n_chk):
            while (bi, c, gi) not in done:
                yield
            rows = slice(c * CH, (c + 1) * CH)
            zz = _bdot(jnp.concatenate([ab_sc[bi, 0, rows, ln], strm_sc[bi, 1, rows, ln]], axis=0), st)
            yield
            dmb = (af_sc[bi, 0, rows, ln] + zz[:CH]).astype(BF16)
            upd = jnp.dot(bkt_sc[bi, c, gi], jnp.concatenate([dmb, strm_sc[bi, 6, rows, ln]], axis=0),
                          preferred_element_type=F32)
            yd = _bdot(ab_sc[bi, 1, rows, ln], _block_diag(dmb, blk))
            yield
            st = gcol_sc[bi, c, gi] * st + upd * blk_f
            y_sc[bi, rows, ln] = zz[CH:] + af_sc[bi, 1, rows, ln] + yd
        s_sc[bi, gi] = st

    def stage_tail(bi):
        yw = y_sc[bi]
        mu_h = _seg_sum(yw, ones_bd) * (1.0 / HEAD)
        dy = yw - mu_h
        var_h = _seg_sum(dy * dy, ones_bd) * (1.0 / HEAD)
        yn = dy * lax.rsqrt(var_h + GN_EPS) * lg_ref[...] + lb_ref[...]
        y_rwkv = (yn + bonus_sc[bi]) * g_sc[bi]
        o_ref[bi] = (x_ref[bi] + jnp.dot(yc_sc[bi], wout_ref[0:CONV_W, :], preferred_element_type=F32)
                     + _bdot(y_rwkv, wout_ref[CONV_W:, :]))

    for bi in range(NB):
        stage_inproj(bi)
    for bi in range(NB):
        stage_conv(bi)
        stage_prep(bi)
    a_chains = [chain_a(bi, c, gi) for c in range(n_chk) for bi in range(NB) for gi in range(n_grp)]
    b_chains = [chain_b(bi, gi) for bi in range(NB) for gi in range(n_grp)]
    _interleave(a_chains, b_chains, A_WIDTH)
    for bi in range(NB):
        stage_tail(bi)


def _const_spec(shape):
    nd = len(shape)
    return pl.BlockSpec(shape, lambda *_: (0,) * nd, pipeline_mode=pl.Buffered(1))


def _mixer(x, gmix, win, cw, cb, clg, clb, mu, w0, wd, a0, wa, wg, kkw, ka, rk, lg, lb, wout, tri, ones_bd, blk):
    B, S, D = x.shape
    consts = (gmix, win, cw, cb, clg, clb, mu, w0, wd, a0, wa, wg, kkw, ka, rk, lg, lb, wout, tri, ones_bd, blk)
    return pl.pallas_call(
        _mixer_kernel,
        grid=(B // NB, S // TT),
        in_specs=[pl.BlockSpec((NB, TT, D), lambda b, j: (b, j, 0))] + [_const_spec(c.shape) for c in consts],
        out_specs=pl.BlockSpec((NB, TT, D), lambda b, j: (b, j, 0)),
        out_shape=jax.ShapeDtypeStruct((B, S, D), F32),
        scratch_shapes=[
            pltpu.VMEM((NB, TT, P_COLS), F32),
            pltpu.VMEM((NB, 8, RWKV_P), F32),
            pltpu.VMEM((NB, HALO + TT, CONV_W), F32),
            pltpu.VMEM((NB, 7, HALO + TT, CONV_W), F32),
            pltpu.VMEM((NB, TT, CONV_W), BF16),
            pltpu.VMEM((NB, TT, RW), F32),
            pltpu.VMEM((NB, TT, RW), F32),
            pltpu.VMEM((NB, 7, TT, RW), BF16),
            pltpu.VMEM((NB, 2, TT, RW), F32),
            pltpu.VMEM((NB, 2, TT, RW), BF16),
            pltpu.VMEM((NB, TT // CH, RW // GW, GW, 2 * CH), BF16),
            pltpu.VMEM((NB, TT // CH, RW // GW, GW, GW), F32),
            pltpu.VMEM((NB, TT, RW), F32),
            pltpu.VMEM((NB, TT, RW), F32),
            pltpu.VMEM((NB, RW // GW, GW, GW), F32),
        ],
        compiler_params=pltpu.CompilerParams(
            dimension_semantics=("arbitrary", "arbitrary"), vmem_limit_bytes=VMEM_LIMIT),
        name="mixer",
    )(x, *consts)


def _xattn_kernel(x_ref, mem_ref, gc_ref, gm_ref, wq_ref, wk_ref, wv_ref, wo_ref, o_ref, k_sc, v_sc):
    @pl.when(pl.program_id(1) == 0)
    def _():
        mn = _rms(mem_ref[0], gm_ref[...]).astype(BF16)
        k_sc[...] = jnp.dot(mn, wk_ref[...], preferred_element_type=F32).astype(BF16)
        v_sc[...] = jnp.dot(mn, wv_ref[...], preferred_element_type=F32).astype(BF16)

    heads = [slice(hd * XHD, (hd + 1) * XHD) for hd in range(N_XH)]
    rows = [slice(i * (TQ // 2), (i + 1) * (TQ // 2)) for i in range(2)]
    xs = [x_ref[0, r, :] for r in rows]
    qs = [_bdot(_rms(x, gc_ref[...]), wq_ref[...]).astype(BF16) for x in xs]
    scores = [[_bdot_nt(q[:, sl], k_sc[:, sl]) * (XHD ** -0.5) for sl in heads] for q in qs]
    probs = []
    for per_head in scores:
        ps = []
        for s in per_head:
            e = jnp.exp(s - jnp.max(s, axis=-1, keepdims=True))
            ps.append((e / jnp.sum(e, axis=-1, keepdims=True)).astype(BF16))
        probs.append(ps)
    for r, x, ps in zip(rows, xs, probs):
        outs = [jnp.dot(p, v_sc[:, sl], preferred_element_type=F32) for p, sl in zip(ps, heads)]
        o_ref[0, r, :] = x + _bdot(jnp.concatenate(outs, axis=1), wo_ref[...])


def _xattn(x, mem, gc, gm, wq, wk, wv, wo):
    B, S, D = x.shape
    M = mem.shape[1]
    consts = (gc, gm, wq, wk, wv, wo)
    return pl.pallas_call(
        _xattn_kernel,
        grid=(B, S // TQ),
        in_specs=[pl.BlockSpec((1, TQ, D), lambda b, j: (b, j, 0)),
                  pl.BlockSpec((1, M, D), lambda b, j: (b, 0, 0))] + [_const_spec(c.shape) for c in consts],
        out_specs=pl.BlockSpec((1, TQ, D), lambda b, j: (b, j, 0)),
        out_shape=jax.ShapeDtypeStruct((B, S, D), F32),
        scratch_shapes=[pltpu.VMEM((M, D), BF16), pltpu.VMEM((M, D), BF16)],
        compiler_params=pltpu.CompilerParams(
            dimension_semantics=("arbitrary", "arbitrary"), vmem_limit_bytes=VMEM_LIMIT),
        name="xattn",
    )(x, mem, *consts)


def _ffn_kernel(x_ref, gf_ref, w1_ref, w2_ref, gfin_ref, o_ref, *, final_norm):
    rows = [slice(i * (TM // 2), (i + 1) * (TM // 2)) for i in range(2)]
    accs = [x_ref[r, :] for r in rows]
    hs = [_rms(a, gf_ref[...]).astype(BF16) for a in accs]
    for c in range(D_FF // FF_CH):
        sl = slice(c * FF_CH, (c + 1) * FF_CH)
        ts = [jnp.maximum(jnp.dot(h, w1_ref[:, sl], preferred_element_type=F32), 0.0) for h in hs]
        accs = [a + _bdot(t * t, w2_ref[sl, :]) for a, t in zip(accs, ts)]
    for r, a in zip(rows, accs):
        o_ref[r, :] = _rms(a, gfin_ref[...]) if final_norm else a


def _ffn(x2d, gf, w1, w2, gfin, final_norm):
    M, D = x2d.shape
    consts = (gf, w1, w2, gfin)
    return pl.pallas_call(
        functools.partial(_ffn_kernel, final_norm=final_norm),
        grid=(M // TM,),
        in_specs=[pl.BlockSpec((TM, D), lambda i: (i, 0))] + [_const_spec(c.shape) for c in consts],
        out_specs=pl.BlockSpec((TM, D), lambda i: (i, 0)),
        out_shape=jax.ShapeDtypeStruct((M, D), F32),
        compiler_params=pltpu.CompilerParams(
            dimension_semantics=("arbitrary",), vmem_limit_bytes=VMEM_LIMIT),
        name="ffn",
    )(x2d, *consts)


def _row(v):
    return v.reshape(1, -1).astype(F32)


def _mixer_constants():
    i = jnp.arange(TT)
    tri = ((i[:, None] // CH == i[None, :] // CH) & (i[:, None] >= i[None, :])).astype(BF16)
    jj = jnp.arange(GW)
    blk = (jj[:, None] // HEAD == jj[None, :] // HEAD).astype(BF16)
    return tri, blk, blk


def kernel(x, mem, g_mix, w_in, conv_w, conv_b, conv_ln_g, conv_ln_b, mu_b, w0, w_decay2, a0, a_lora2,
           g_lora2, k_k, k_a, r_k, lnx_g, lnx_b, w_out, g_cross, g_mem, wq, wk, wv, wo, g_ffn, w_ff1, w_ff2,
           g_final):
    B, S, D = x.shape
    depth = w_in.shape[0]
    tri, ones_bd, blk = _mixer_constants()
    pad_cols = LORA_PAD - (DEC_L + AAA_L + GATE_L)
    for l in range(depth):
        win = jnp.pad(w_in[l], ((0, 0), (0, pad_cols))).astype(BF16)
        mu = _row(jnp.pad(mu_b[l], (0, pad_cols)))
        wd = jnp.zeros((LORA_PAD, RW), F32).at[0:DEC_L].set(w_decay2[l]).astype(BF16)
        wa = jnp.zeros((LORA_PAD, RW), F32).at[DEC_L:DEC_L + AAA_L].set(a_lora2[l]).astype(BF16)
        wg = jnp.zeros((LORA_PAD, RW), F32).at[DEC_L + AAA_L:DEC_L + AAA_L + GATE_L].set(g_lora2[l]).astype(BF16)
        x = _mixer(x, _row(g_mix[l]), win, conv_w[l].astype(F32), _row(conv_b[l]), _row(conv_ln_g[l]),
                   _row(conv_ln_b[l]), mu, _row(w0[l]), wd, _row(a0[l]), wa, wg, _row(k_k[l]), _row(k_a[l]),
                   _row(r_k[l]), _row(lnx_g[l]), _row(lnx_b[l]), w_out[l].astype(BF16), tri, ones_bd, blk)
        x = _xattn(x, mem, _row(g_cross[l]), _row(g_mem[l]), wq[l].astype(BF16), wk[l].astype(BF16),
                   wv[l].astype(BF16), wo[l].astype(BF16))
        x = _ffn(x.reshape(B * S, D), _row(g_ffn[l]), w_ff1[l].astype(BF16), w_ff2[l].astype(BF16),
                 _row(g_final), final_norm=(l == depth - 1)).reshape(B, S, D)
    return x
```

```python
import functools

import jax
import jax.numpy as jnp
from jax import lax
from jax.experimental import pallas as pl
from jax.experimental.pallas import tpu as pltpu

F32 = jnp.float32
BF16 = jnp.bfloat16

D_MODEL = 1024
CONV_W = 512
CONV_K = 31
RW = 512
HEAD = 64
LORA_PAD = 256
DEC_L, AAA_L, GATE_L = 32, 32, 96
P_COLS = 2 * CONV_W + 3 * RW + LORA_PAD
RWKV_P = 3 * RW + LORA_PAD
N_XH = 4
XHD = D_MODEL // N_XH
MEM_LEN = 256
D_FF = 4 * D_MODEL
RMS_EPS = 1e-6
LN_EPS = 1e-5
GN_EPS = 1e-5 * HEAD

CH = 64
GRP = 4
GW = GRP * HEAD
TT = 256
NB = 2
A_WIDTH = 16
HALO = 32
CONV_RB = 32
TQ = 512
TM = 1024
FF_CH = 1024
VMEM_LIMIT = 56 * 1024 * 1024


def _bdot(a, b):
    return jnp.dot(a.astype(BF16), b.astype(BF16), preferred_element_type=F32)


def _bdot_nt(a, b):
    return lax.dot_general(a.astype(BF16), b.astype(BF16), (((1,), (1,)), ((), ())),
                           preferred_element_type=F32)


def _rms(x, g):
    return x * lax.rsqrt(jnp.mean(x * x, axis=-1, keepdims=True) + RMS_EPS) * g


def _seg_sum(x, ones_bd):
    hi = x.astype(BF16)
    outs = []
    for g in range(RW // GW):
        sl = slice(g * GW, (g + 1) * GW)
        outs.append(jnp.dot(hi[:, sl], ones_bd, preferred_element_type=F32))
    return jnp.concatenate(outs, axis=1)


def _block_diag(z, blk):
    return jnp.concatenate([z] * GRP, axis=0) * blk


def _interleave(a_chains, b_chains, width):
    pending = list(a_chains)
    live_a, live_b = [], list(b_chains)
    while pending or live_a or live_b:
        while pending and len(live_a) < width:
            live_a.append(pending.pop(0))
        nxt_a, nxt_b = [], []
        for group, nxt in ((live_a, nxt_a), (live_b, nxt_b)):
            for ch in group:
                try:
                    next(ch)
                    nxt.append(ch)
                except StopIteration:
                    pass
        live_a, live_b = nxt_a, nxt_b


def _mixer_kernel(x_ref, gmix_ref, win_ref, cw_ref, cb_ref, clg_ref, clb_ref, mu_ref, w0_ref, wd_ref,
                  a0_ref, wa_ref, wg_ref, kkw_ref, ka_ref, rk_ref, lg_ref, lb_ref, wout_ref,
                  tri_ref, ones_ref, blk_ref,
                  o_ref,
                  p_sc, prev_sc, u_sc, ush_sc, yc_sc, g_sc, bonus_sc, strm_sc, af_sc, ab_sc, bkt_sc, gcol_sc, gam_sc, y_sc,
                  s_sc):
    j = pl.program_id(1)

    @pl.when(j == 0)
    def _():
        prev_sc[...] = jnp.zeros(prev_sc.shape, F32)
        u_sc[:, 0:HALO, :] = jnp.zeros((NB, HALO, CONV_W), F32)
        s_sc[...] = jnp.zeros(s_sc.shape, F32)

    ones_bd = ones_ref[...]
    blk = blk_ref[...]
    blk_f = blk.astype(F32)
    ri = lax.broadcasted_iota(jnp.int32, (CH, GW), 0)
    ci = lax.broadcasted_iota(jnp.int32, (CH, GW), 1) % CH
    tri_s = ri > ci
    tri_i = ri >= ci
    eye_cat = (ri == ci).astype(F32)
    n_grp = RW // GW
    n_chk = TT // CH
    done = set()

    def stage_inproj(bi):
        h = _rms(x_ref[bi], gmix_ref[...])
        p_sc[bi] = _bdot(h, win_ref[...])

    def stage_conv(bi):
        u_sc[bi, HALO:HALO + TT, :] = p_sc[bi, :, 0:CONV_W] * jax.nn.sigmoid(p_sc[bi, :, CONV_W:2 * CONV_W])
        ua = u_sc[bi]
        for sh in range(1, 8):
            ush_sc[bi, sh - 1] = pltpu.roll(ua, HALO + TT - sh, axis=0)
        off = HALO - (CONV_K - 1)
        for rb in range(TT // CONV_RB):
            base = rb * CONV_RB
            acc = jnp.broadcast_to(cb_ref[...], (CONV_RB, CONV_W))
            for t in range(CONV_K):
                m8, sh = divmod(off + t, 8)
                lo = base + 8 * m8
                src = u_sc[bi, lo:lo + CONV_RB, :] if sh == 0 else ush_sc[bi, sh - 1, lo:lo + CONV_RB, :]
                acc = acc + cw_ref[t:t + 1, :] * src
            m = jnp.mean(acc, axis=-1, keepdims=True)
            dlt = acc - m
            var = jnp.mean(dlt * dlt, axis=-1, keepdims=True)
            y_conv = jax.nn.silu(dlt * lax.rsqrt(var + LN_EPS) * clg_ref[...] + clb_ref[...])
            yc_sc[bi, base:base + CONV_RB, :] = y_conv.astype(BF16)
        u_sc[bi, 0:HALO, :] = u_sc[bi, TT:TT + HALO, :]

    def stage_prep(bi):
        pb = p_sc[bi, :, 2 * CONV_W:]
        rolled = pltpu.roll(pb, 1, axis=0)
        first = lax.broadcasted_iota(jnp.int32, (8, RWKV_P), 0) == 0
        prev = jnp.concatenate([jnp.where(first, prev_sc[bi, 0:1, :], rolled[0:8]), rolled[8:]], axis=0)
        prev_sc[bi, 0:1, :] = pb[TT - 1:TT, :]
        z = pb + mu_ref[...] * (prev - pb)
        r = z[:, 0:RW]
        k = z[:, RW:2 * RW]
        v = z[:, 2 * RW:3 * RW]
        zl = z[:, 3 * RW:]
        dec = _bdot(jnp.tanh(zl), wd_ref[...])
        w_log = -jax.nn.softplus(-(w0_ref[...] + dec)) - 0.5
        lw = -jnp.exp(w_log)
        a = jax.nn.sigmoid(a0_ref[...] + _bdot(zl, wa_ref[...]))
        g_sc[bi] = _bdot(jax.nn.sigmoid(zl), wg_ref[...])
        kk = k * kkw_ref[...]
        kk = kk / jnp.maximum(jnp.sqrt(_seg_sum(kk * kk, ones_bd)), 1e-12)
        k = k * (1.0 + (a - 1.0) * ka_ref[...])
        bonus_sc[bi] = _seg_sum(r * k * rk_ref[...], ones_bd) * v
        l1 = lw.astype(BF16)
        l2 = (lw - l1.astype(F32)).astype(BF16)
        tri = tri_ref[...]
        cs = jnp.dot(tri, l1, preferred_element_type=F32) + jnp.dot(tri, l2, preferred_element_type=F32)
        cl = jnp.concatenate(
            [jnp.broadcast_to(cs[c * CH + CH - 1:c * CH + CH, :], (CH, RW)) for c in range(n_chk)], axis=0)
        e_in = jnp.exp(cs)
        e_inv = jnp.exp(-cs)
        e_end = jnp.exp(cl - cs)
        kka = kk * a
        strm_sc[bi, 0] = (kk * jnp.exp(cs - lw)).astype(BF16)
        strm_sc[bi, 1] = (r * e_in).astype(BF16)
        strm_sc[bi, 2] = (kka * e_inv).astype(BF16)
        strm_sc[bi, 3] = (k * e_inv).astype(BF16)
        strm_sc[bi, 4] = (kka * e_end).astype(BF16)
        strm_sc[bi, 5] = (k * e_end).astype(BF16)
        strm_sc[bi, 6] = v.astype(BF16)
        gam_sc[bi] = jnp.exp(cl)

    def chain_a(bi, c, gi):
        rows = slice(c * CH, (c + 1) * CH)
        ln = slice(gi * GW, (gi + 1) * GW)
        kt = strm_sc[bi, 0, rows, ln]
        lhs = jnp.concatenate([kt, strm_sc[bi, 1, rows, ln]], axis=0)
        gb = _bdot_nt(lhs, _block_diag(strm_sc[bi, 2, rows, ln], blk))
        gk = _bdot_nt(lhs, _block_diag(strm_sc[bi, 3, rows, ln], blk))
        yield
        pm = jnp.where(tri_s, -gb[:CH], 0.0)
        lk = jnp.where(tri_s, gk[:CH], 0.0)
        ab_sc[bi, 1, rows, ln] = jnp.where(tri_i, gb[CH:], 0.0).astype(BF16)
        ark = jnp.where(tri_i, gk[CH:], 0.0)
        w = eye_cat + pm
        q = _bdot(pm, _block_diag(pm.astype(BF16), blk))
        yield
        for _ in range(4):
            rr = _bdot(jnp.concatenate([w, q], axis=0), _block_diag(q.astype(BF16), blk))
            yield
            w = w + rr[:CH]
            q = rr[CH:]
        wq = _bdot(w, _block_diag(q.astype(BF16), blk))
        xy = _bdot(jnp.concatenate([lk, ark], axis=0), _block_diag(strm_sc[bi, 6, rows, ln], blk))
        yield
        wb = (w + wq).astype(BF16)
        af_sc[bi, 1, rows, ln] = xy[CH:]
        u_ = _bdot(wb, _block_diag(xy[:CH].astype(BF16), blk))
        wk = _bdot(wb, _block_diag(kt, blk))
        bk = jnp.concatenate([strm_sc[bi, 4, rows, ln], strm_sc[bi, 5, rows, ln]], axis=0).astype(F32)
        bkt_sc[bi, c, gi] = bk.T.astype(BF16)
        gcol_sc[bi, c, gi] = jnp.broadcast_to(gam_sc[bi, c * CH:c * CH + 1, ln], (GW, GW)).T
        yield
        af_sc[bi, 0, rows, ln] = -u_
        ab_sc[bi, 0, rows, ln] = (-wk).astype(BF16)
        done.add((bi, c, gi))

    def chain_b(bi, gi):
        ln = slice(gi * GW, (gi + 1) * GW)
        st = s_sc[bi, gi]
        for c in range(n_chk):
            while (bi, c, gi) not in done:
                yield
            rows = slice(c * CH, (c + 1) * CH)
            zz = _bdot(jnp.concatenate([ab_sc[bi, 0, rows, ln], strm_sc[bi, 1, rows, ln]], axis=0), st)
            yield
            dmb = (af_sc[bi, 0, rows, ln] + zz[:CH]).astype(BF16)
            upd = jnp.dot(bkt_sc[bi, c, gi], jnp.concatenate([dmb, strm_sc[bi, 6, rows, ln]], axis=0),
                          preferred_element_type=F32)
            yd = _bdot(ab_sc[bi, 1, rows, ln], _block_diag(dmb, blk))
            yield
            st = gcol_sc[bi, c, gi] * st + upd * blk_f
            y_sc[bi, rows, ln] = zz[CH:] + af_sc[bi, 1, rows, ln] + yd
        s_sc[bi, gi] = st

    def stage_tail(bi):
        yw = y_sc[bi]
        mu_h = _seg_sum(yw, ones_bd) * (1.0 / HEAD)
        dy = yw - mu_h
        var_h = _seg_sum(dy * dy, ones_bd) * (1.0 / HEAD)
        yn = dy * lax.rsqrt(var_h + GN_EPS) * lg_ref[...] + lb_ref[...]
        y_rwkv = (yn + bonus_sc[bi]) * g_sc[bi]
        o_ref[bi] = (x_ref[bi] + jnp.dot(yc_sc[bi], wout_ref[0:CONV_W, :], preferred_element_type=F32)
                     + _bdot(y_rwkv, wout_ref[CONV_W:, :]))

    for bi in range(NB):
        stage_inproj(bi)
    for bi in range(NB):
        stage_conv(bi)
        stage_prep(bi)
    a_chains = [chain_a(bi, c, gi) for c in range(n_chk) for bi in range(NB) for gi in range(n_grp)]
    b_chains = [chain_b(bi, gi) for bi in range(NB) for gi in range(n_grp)]
    _interleave(a_chains, b_chains, A_WIDTH)
    for bi in range(NB):
        stage_tail(bi)


def _const_spec(shape):
    nd = len(shape)
    return pl.BlockSpec(shape, lambda *_: (0,) * nd, pipeline_mode=pl.Buffered(1))


def _mixer(x, gmix, win, cw, cb, clg, clb, mu, w0, wd, a0, wa, wg, kkw, ka, rk, lg, lb, wout, tri, ones_bd, blk):
    B, S, D = x.shape
    consts = (gmix, win, cw, cb, clg, clb, mu, w0, wd, a0, wa, wg, kkw, ka, rk, lg, lb, wout, tri, ones_bd, blk)
    return pl.pallas_call(
        _mixer_kernel,
        grid=(B // NB, S // TT),
        in_specs=[pl.BlockSpec((NB, TT, D), lambda b, j: (b, j, 0))] + [_const_spec(c.shape) for c in consts],
        out_specs=pl.BlockSpec((NB, TT, D), lambda b, j: (b, j, 0)),
        out_shape=jax.ShapeDtypeStruct((B, S, D), F32),
        scratch_shapes=[
            pltpu.VMEM((NB, TT, P_COLS), F32),
            pltpu.VMEM((NB, 8, RWKV_P), F32),
            pltpu.VMEM((NB, HALO + TT, CONV_W), F32),
            pltpu.VMEM((NB, 7, HALO + TT, CONV_W), F32),
            pltpu.VMEM((NB, TT, CONV_W), BF16),
            pltpu.VMEM((NB, TT, RW), F32),
            pltpu.VMEM((NB, TT, RW), F32),
            pltpu.VMEM((NB, 7, TT, RW), BF16),
            pltpu.VMEM((NB, 2, TT, RW), F32),
            pltpu.VMEM((NB, 2, TT, RW), BF16),
            pltpu.VMEM((NB, TT // CH, RW // GW, GW, 2 * CH), BF16),
            pltpu.VMEM((NB, TT // CH, RW // GW, GW, GW), F32),
            pltpu.VMEM((NB, TT, RW), F32),
            pltpu.VMEM((NB, TT, RW), F32),
            pltpu.VMEM((NB, RW // GW, GW, GW), F32),
        ],
        compiler_params=pltpu.CompilerParams(
            dimension_semantics=("arbitrary", "arbitrary"), vmem_limit_bytes=VMEM_LIMIT),
        name="mixer",
    )(x, *consts)


def _xattn_kernel(x_ref, mem_ref, gc_ref, gm_ref, wq_ref, wk_ref, wv_ref, wo_ref, o_ref, k_sc, v_sc):
    @pl.when(pl.program_id(1) == 0)
    def _():
        mn = _rms(mem_ref[0], gm_ref[...]).astype(BF16)
        k_sc[...] = jnp.dot(mn, wk_ref[...], preferred_element_type=F32).astype(BF16)
        v_sc[...] = jnp.dot(mn, wv_ref[...], preferred_element_type=F32).astype(BF16)

    heads = [slice(hd * XHD, (hd + 1) * XHD) for hd in range(N_XH)]
    rows = [slice(i * (TQ // 2), (i + 1) * (TQ // 2)) for i in range(2)]
    xs = [x_ref[0, r, :] for r in rows]
    qs = [_bdot(_rms(x, gc_ref[...]), wq_ref[...]).astype(BF16) for x in xs]
    scores = [[_bdot_nt(q[:, sl], k_sc[:, sl]) * (XHD ** -0.5) for sl in heads] for q in qs]
    probs = []
    for per_head in scores:
        ps = []
        for s in per_head:
            e = jnp.exp(s - jnp.max(s, axis=-1, keepdims=True))
            ps.append((e / jnp.sum(e, axis=-1, keepdims=True)).astype(BF16))
        probs.append(ps)
    for r, x, ps in zip(rows, xs, probs):
        outs = [jnp.dot(p, v_sc[:, sl], preferred_element_type=F32) for p, sl in zip(ps, heads)]
        o_ref[0, r, :] = x + _bdot(jnp.concatenate(outs, axis=1), wo_ref[...])


def _xattn(x, mem, gc, gm, wq, wk, wv, wo):
    B, S, D = x.shape
    M = mem.shape[1]
    consts = (gc, gm, wq, wk, wv, wo)
    return pl.pallas_call(
        _xattn_kernel,
        grid=(B, S // TQ),
        in_specs=[pl.BlockSpec((1, TQ, D), lambda b, j: (b, j, 0)),
                  pl.BlockSpec((1, M, D), lambda b, j: (b, 0, 0))] + [_const_spec(c.shape) for c in consts],
        out_specs=pl.BlockSpec((1, TQ, D), lambda b, j: (b, j, 0)),
        out_shape=jax.ShapeDtypeStruct((B, S, D), F32),
        scratch_shapes=[pltpu.VMEM((M, D), BF16), pltpu.VMEM((M, D), BF16)],
        compiler_params=pltpu.CompilerParams(
            dimension_semantics=("arbitrary", "arbitrary"), vmem_limit_bytes=VMEM_LIMIT),
        name="xattn",
    )(x, mem, *consts)


def _ffn_kernel(x_ref, gf_ref, w1_ref, w2_ref, gfin_ref, o_ref, *, final_norm):
    rows = [slice(i * (TM // 2), (i + 1) * (TM // 2)) for i in range(2)]
    accs = [x_ref[r, :] for r in rows]
    hs = [_rms(a, gf_ref[...]).astype(BF16) for a in accs]
    for c in range(D_FF // FF_CH):
        sl = slice(c * FF_CH, (c + 1) * FF_CH)
        ts = [jnp.maximum(jnp.dot(h, w1_ref[:, sl], preferred_element_type=F32), 0.0) for h in hs]
        accs = [a + _bdot(t * t, w2_ref[sl, :]) for a, t in zip(accs, ts)]
    for r, a in zip(rows, accs):
        o_ref[r, :] = _rms(a, gfin_ref[...]) if final_norm else a


def _ffn(x2d, gf, w1, w2, gfin, final_norm):
    M, D = x2d.shape
    consts = (gf, w1, w2, gfin)
    return pl.pallas_call(
        functools.partial(_ffn_kernel, final_norm=final_norm),
        grid=(M // TM,),
        in_specs=[pl.BlockSpec((TM, D), lambda i: (i, 0))] + [_const_spec(c.shape) for c in consts],
        out_specs=pl.BlockSpec((TM, D), lambda i: (i, 0)),
        out_shape=jax.ShapeDtypeStruct((M, D), F32),
        compiler_params=pltpu.CompilerParams(
            dimension_semantics=("arbitrary",), vmem_limit_bytes=VMEM_LIMIT),
        name="ffn",
    )(x2d, *consts)


def _row(v):
    return v.reshape(1, -1).astype(F32)


def _mixer_constants():
    i = jnp.arange(TT)
    tri = ((i[:, None] // CH == i[None, :] // CH) & (i[:, None] >= i[None, :])).astype(BF16)
    jj = jnp.arange(GW)
    blk = (jj[:, None] // HEAD == jj[None, :] // HEAD).astype(BF16)
    return tri, blk, blk


def kernel(x, mem, g_mix, w_in, conv_w, conv_b, conv_ln_g, conv_ln_b, mu_b, w0, w_decay2, a0, a_lora2,
           g_lora2, k_k, k_a, r_k, lnx_g, lnx_b, w_out, g_cross, g_mem, wq, wk, wv, wo, g_ffn, w_ff1, w_ff2,
           g_final):
    B, S, D = x.shape
    depth = w_in.shape[0]
    tri, ones_bd, blk = _mixer_constants()
    pad_cols = LORA_PAD - (DEC_L + AAA_L + GATE_L)
    for l in range(depth):
        win = jnp.pad(w_in[l], ((0, 0), (0, pad_cols))).astype(BF16)
        mu = _row(jnp.pad(mu_b[l], (0, pad_cols)))
        wd = jnp.zeros((LORA_PAD, RW), F32).at[0:DEC_L].set(w_decay2[l]).astype(BF16)
        wa = jnp.zeros((LORA_PAD, RW), F32).at[DEC_L:DEC_L + AAA_L].set(a_lora2[l]).astype(BF16)
        wg = jnp.zeros((LORA_PAD, RW), F32).at[DEC_L + AAA_L:DEC_L + AAA_L + GATE_L].set(g_lora2[l]).astype(BF16)
        x = _mixer(x, _row(g_mix[l]), win, conv_w[l].astype(F32), _row(conv_b[l]), _row(conv_ln_g[l]),
                   _row(conv_ln_b[l]), mu, _row(w0[l]), wd, _row(a0[l]), wa, wg, _row(k_k[l]), _row(k_a[l]),
                   _row(r_k[l]), _row(lnx_g[l]), _row(lnx_b[l]), w_out[l].astype(BF16), tri, ones_bd, blk)
        x = _xattn(x, mem, _row(g_cross[l]), _row(g_mem[l]), wq[l].astype(BF16), wk[l].astype(BF16),
                   wv[l].astype(BF16), wo[l].astype(BF16))
        x = _ffn(x.reshape(B * S, D), _row(g_ffn[l]), w_ff1[l].astype(BF16), w_ff2[l].astype(BF16),
                 _row(g_final), final_norm=(l == depth - 1)).reshape(B, S, D)
    return x
```

```python
import functools

import jax
import jax.numpy as jnp
from jax import lax
from jax.experimental import pallas as pl
from jax.experimental.pallas import tpu as pltpu

F32 = jnp.float32
BF16 = jnp.bfloat16

D_MODEL = 1024
CONV_W = 512
CONV_K = 31
RW = 512
HEAD = 64
LORA_PAD = 256
DEC_L, AAA_L, GATE_L = 32, 32, 96
P_COLS = 2 * CONV_W + 3 * RW + LORA_PAD
RWKV_P = 3 * RW + LORA_PAD
N_XH = 4
XHD = D_MODEL // N_XH
MEM_LEN = 256
D_FF = 4 * D_MODEL
RMS_EPS = 1e-6
LN_EPS = 1e-5
GN_EPS = 1e-5 * HEAD

CH = 64
GRP = 4
GW = GRP * HEAD
TT = 256
NB = 2
A_WIDTH = 8
HALO = 32
CONV_RB = 32
TQ = 512
TM = 1024
FF_CH = 1024
VMEM_LIMIT = 56 * 1024 * 1024


def _bdot(a, b):
    return jnp.dot(a.astype(BF16), b.astype(BF16), preferred_element_type=F32)


def _bdot_nt(a, b):
    return lax.dot_general(a.astype(BF16), b.astype(BF16), (((1,), (1,)), ((), ())),
                           preferred_element_type=F32)


def _rms(x, g):
    return x * lax.rsqrt(jnp.mean(x * x, axis=-1, keepdims=True) + RMS_EPS) * g


def _seg_sum(x, ones_bd):
    hi = x.astype(BF16)
    outs = []
    for g in range(RW // GW):
        sl = slice(g * GW, (g + 1) * GW)
        outs.append(jnp.dot(hi[:, sl], ones_bd, preferred_element_type=F32))
    return jnp.concatenate(outs, axis=1)


def _block_diag(z, blk):
    return jnp.concatenate([z] * GRP, axis=0) * blk


def _interleave(a_chains, b_chains, width):
    pending = list(a_chains)
    live_a, live_b = [], list(b_chains)
    while pending or live_a or live_b:
        while pending and len(live_a) < width:
            live_a.append(pending.pop(0))
        nxt_a, nxt_b = [], []
        for group, nxt in ((live_a, nxt_a), (live_b, nxt_b)):
            for ch in group:
                try:
                    next(ch)
                    nxt.append(ch)
                except StopIteration:
                    pass
        live_a, live_b = nxt_a, nxt_b


def _mixer_kernel(x_ref, gmix_ref, win_ref, cw_ref, cb_ref, clg_ref, clb_ref, mu_ref, w0_ref, wd_ref,
                  a0_ref, wa_ref, wg_ref, kkw_ref, ka_ref, rk_ref, lg_ref, lb_ref, wout_ref,
                  tri_ref, ones_ref, blk_ref,
                  o_ref,
                  p_sc, prev_sc, u_sc, ush_sc, yc_sc, g_sc, bonus_sc, strm_sc, af_sc, ab_sc, bkt_sc, gcol_sc, gam_sc, y_sc,
                  s_sc):
    j = pl.program_id(1)

    @pl.when(j == 0)
    def _():
        prev_sc[...] = jnp.zeros(prev_sc.shape, F32)
        u_sc[:, 0:HALO, :] = jnp.zeros((NB, HALO, CONV_W), F32)
        s_sc[...] = jnp.zeros(s_sc.shape, F32)

    ones_bd = ones_ref[...]
    blk = blk_ref[...]
    blk_f = blk.astype(F32)
    ri = lax.broadcasted_iota(jnp.int32, (CH, GW), 0)
    ci = lax.broadcasted_iota(jnp.int32, (CH, GW), 1) % CH
    tri_s = ri > ci
    tri_i = ri >= ci
    eye_cat = (ri == ci).astype(F32)
    n_grp = RW // GW
    n_chk = TT // CH
    done = set()

    def stage_inproj(bi):
        h = _rms(x_ref[bi], gmix_ref[...])
        p_sc[bi] = _bdot(h, win_ref[...])

    def stage_conv(bi):
        u_sc[bi, HALO:HALO + TT, :] = p_sc[bi, :, 0:CONV_W] * jax.nn.sigmoid(p_sc[bi, :, CONV_W:2 * CONV_W])
        ua = u_sc[bi]
        for sh in range(1, 8):
            ush_sc[bi, sh - 1] = pltpu.roll(ua, HALO + TT - sh, axis=0)
        off = HALO - (CONV_K - 1)
        for rb in range(TT // CONV_RB):
            base = rb * CONV_RB
            acc = jnp.broadcast_to(cb_ref[...], (CONV_RB, CONV_W))
            for t in range(CONV_K):
                m8, sh = divmod(off + t, 8)
                lo = base + 8 * m8
                src = u_sc[bi, lo:lo + CONV_RB, :] if sh == 0 else ush_sc[bi, sh - 1, lo:lo + CONV_RB, :]
                acc = acc + cw_ref[t:t + 1, :] * src
            m = jnp.mean(acc, axis=-1, keepdims=True)
            dlt = acc - m
            var = jnp.mean(dlt * dlt, axis=-1, keepdims=True)
            y_conv = jax.nn.silu(dlt * lax.rsqrt(var + LN_EPS) * clg_ref[...] + clb_ref[...])
            yc_sc[bi, base:base + CONV_RB, :] = y_conv.astype(BF16)
        u_sc[bi, 0:HALO, :] = u_sc[bi, TT:TT + HALO, :]

    def stage_prep(bi):
        pb = p_sc[bi, :, 2 * CONV_W:]
        rolled = pltpu.roll(pb, 1, axis=0)
        first = lax.broadcasted_iota(jnp.int32, (8, RWKV_P), 0) == 0
        prev = jnp.concatenate([jnp.where(first, prev_sc[bi, 0:1, :], rolled[0:8]), rolled[8:]], axis=0)
        prev_sc[bi, 0:1, :] = pb[TT - 1:TT, :]
        z = pb + mu_ref[...] * (prev - pb)
        r = z[:, 0:RW]
        k = z[:, RW:2 * RW]
        v = z[:, 2 * RW:3 * RW]
        zl = z[:, 3 * RW:]
        dec = _bdot(jnp.tanh(zl), wd_ref[...])
        w_log = -jax.nn.softplus(-(w0_ref[...] + dec)) - 0.5
        lw = -jnp.exp(w_log)
        a = jax.nn.sigmoid(a0_ref[...] + _bdot(zl, wa_ref[...]))
        g_sc[bi] = _bdot(jax.nn.sigmoid(zl), wg_ref[...])
        kk = k * kkw_ref[...]
        kk = kk / jnp.maximum(jnp.sqrt(_seg_sum(kk * kk, ones_bd)), 1e-12)
        k = k * (1.0 + (a - 1.0) * ka_ref[...])
        bonus_sc[bi] = _seg_sum(r * k * rk_ref[...], ones_bd) * v
        l1 = lw.astype(BF16)
        l2 = (lw - l1.astype(F32)).astype(BF16)
        tri = tri_ref[...]
        cs = jnp.dot(tri, l1, preferred_element_type=F32) + jnp.dot(tri, l2, preferred_element_type=F32)
        cl = jnp.concatenate(
            [jnp.broadcast_to(cs[c * CH + CH - 1:c * CH + CH, :], (CH, RW)) for c in range(n_chk)], axis=0)
        e_in = jnp.exp(cs)
        e_inv = jnp.exp(-cs)
        e_end = jnp.exp(cl - cs)
        kka = kk * a
        strm_sc[bi, 0] = (kk * jnp.exp(cs - lw)).astype(BF16)
        strm_sc[bi, 1] = (r * e_in).astype(BF16)
        strm_sc[bi, 2] = (kka * e_inv).astype(BF16)
        strm_sc[bi, 3] = (k * e_inv).astype(BF16)
        strm_sc[bi, 4] = (kka * e_end).astype(BF16)
        strm_sc[bi, 5] = (k * e_end).astype(BF16)
        strm_sc[bi, 6] = v.astype(BF16)
        gam_sc[bi] = jnp.exp(cl)

    def chain_a(bi, c, gi):
        rows = slice(c * CH, (c + 1) * CH)
        ln = slice(gi * GW, (gi + 1) * GW)
        kt = strm_sc[bi, 0, rows, ln]
        lhs = jnp.concatenate([kt, strm_sc[bi, 1, rows, ln]], axis=0)
        gb = _bdot_nt(lhs, _block_diag(strm_sc[bi, 2, rows, ln], blk))
        gk = _bdot_nt(lhs, _block_diag(strm_sc[bi, 3, rows, ln], blk))
        yield
        pm = jnp.where(tri_s, -gb[:CH], 0.0)
        lk = jnp.where(tri_s, gk[:CH], 0.0)
        ab_sc[bi, 1, rows, ln] = jnp.where(tri_i, gb[CH:], 0.0).astype(BF16)
        ark = jnp.where(tri_i, gk[CH:], 0.0)
        w = eye_cat + pm
        q = _bdot(pm, _block_diag(pm.astype(BF16), blk))
        yield
        for _ in range(4):
            rr = _bdot(jnp.concatenate([w, q], axis=0), _block_diag(q.astype(BF16), blk))
            yield
            w = w + rr[:CH]
            q = rr[CH:]
        wq = _bdot(w, _block_diag(q.astype(BF16), blk))
        xy = _bdot(jnp.concatenate([lk, ark], axis=0), _block_diag(strm_sc[bi, 6, rows, ln], blk))
        yield
        wb = (w + wq).astype(BF16)
        af_sc[bi, 1, rows, ln] = xy[CH:]
        u_ = _bdot(wb, _block_diag(xy[:CH].astype(BF16), blk))
        wk = _bdot(wb, _block_diag(kt, blk))
        bk = jnp.concatenate([strm_sc[bi, 4, rows, ln], strm_sc[bi, 5, rows, ln]], axis=0).astype(F32)
        bkt_sc[bi, c, gi] = bk.T.astype(BF16)
        gcol_sc[bi, c, gi] = jnp.broadcast_to(gam_sc[bi, c * CH:c * CH + 1, ln], (GW, GW)).T
        yield
        af_sc[bi, 0, rows, ln] = -u_
        ab_sc[bi, 0, rows, ln] = (-wk).astype(BF16)
        done.add((bi, c, gi))

    def chain_b(bi, gi):
        ln = slice(gi * GW, (gi + 1) * GW)
        st = s_sc[bi, gi]
        for c in range(n_chk):
            while (bi, c, gi) not in done:
                yield
            rows = slice(c * CH, (c + 1) * CH)
            zz = _bdot(jnp.concatenate([ab_sc[bi, 0, rows, ln], strm_sc[bi, 1, rows, ln]], axis=0), st)
            yield
            dmb = (af_sc[bi, 0, rows, ln] + zz[:CH]).astype(BF16)
            upd = jnp.dot(bkt_sc[bi, c, gi], jnp.concatenate([dmb, strm_sc[bi, 6, rows, ln]], axis=0),
                          preferred_element_type=F32)
            yd = _bdot(ab_sc[bi, 1, rows, ln], _block_diag(dmb, blk))
            yield
            st = gcol_sc[bi, c, gi] * st + upd * blk_f
            y_sc[bi, rows, ln] = zz[CH:] + af_sc[bi, 1, rows, ln] + yd
        s_sc[bi, gi] = st

    def stage_tail(bi):
        yw = y_sc[bi]
        mu_h = _seg_sum(yw, ones_bd) * (1.0 / HEAD)
        dy = yw - mu_h
        var_h = _seg_sum(dy * dy, ones_bd) * (1.0 / HEAD)
        yn = dy * lax.rsqrt(var_h + GN_EPS) * lg_ref[...] + lb_ref[...]
        y_rwkv = (yn + bonus_sc[bi]) * g_sc[bi]
        o_ref[bi] = (x_ref[bi] + jnp.dot(yc_sc[bi], wout_ref[0:CONV_W, :], preferred_element_type=F32)
                     + _bdot(y_rwkv, wout_ref[CONV_W:, :]))

    for bi in range(NB):
        stage_inproj(bi)
    for bi in range(NB):
        stage_conv(bi)
        stage_prep(bi)
    a_chains = [chain_a(bi, c, gi) for c in range(n_chk) for bi in range(NB) for gi in range(n_grp)]
    b_chains = [chain_b(bi, gi) for bi in range(NB) for gi in range(n_grp)]
    _interleave(a_chains, b_chains, A_WIDTH)
    for bi in range(NB):
        stage_tail(bi)


def _const_spec(shape):
    nd = len(shape)
    return pl.BlockSpec(shape, lambda *_: (0,) * nd, pipeline_mode=pl.Buffered(1))


def _mixer(x, gmix, win, cw, cb, clg, clb, mu, w0, wd, a0, wa, wg, kkw, ka, rk, lg, lb, wout, tri, ones_bd, blk):
    B, S, D = x.shape
    consts = (gmix, win, cw, cb, clg, clb, mu, w0, wd, a0, wa, wg, kkw, ka, rk, lg, lb, wout, tri, ones_bd, blk)
    return pl.pallas_call(
        _mixer_kernel,
        grid=(B // NB, S // TT),
        in_specs=[pl.BlockSpec((NB, TT, D), lambda b, j: (b, j, 0))] + [_const_spec(c.shape) for c in consts],
        out_specs=pl.BlockSpec((NB, TT, D), lambda b, j: (b, j, 0)),
        out_shape=jax.ShapeDtypeStruct((B, S, D), F32),
        scratch_shapes=[
            pltpu.VMEM((NB, TT, P_COLS), F32),
            pltpu.VMEM((NB, 8, RWKV_P), F32),
            pltpu.VMEM((NB, HALO + TT, CONV_W), F32),
            pltpu.VMEM((NB, 7, HALO + TT, CONV_W), F32),
            pltpu.VMEM((NB, TT, CONV_W), BF16),
            pltpu.VMEM((NB, TT, RW), F32),
            pltpu.VMEM((NB, TT, RW), F32),
            pltpu.VMEM((NB, 7, TT, RW), BF16),
            pltpu.VMEM((NB, 2, TT, RW), F32),
            pltpu.VMEM((NB, 2, TT, RW), BF16),
            pltpu.VMEM((NB, TT // CH, RW // GW, GW, 2 * CH), BF16),
            pltpu.VMEM((NB, TT // CH, RW // GW, GW, GW), F32),
            pltpu.VMEM((NB, TT, RW), F32),
            pltpu.VMEM((NB, TT, RW), F32),
            pltpu.VMEM((NB, RW // GW, GW, GW), F32),
        ],
        compiler_params=pltpu.CompilerParams(
            dimension_semantics=("arbitrary", "arbitrary"), vmem_limit_bytes=VMEM_LIMIT),
        name="mixer",
    )(x, *consts)


def _xattn_kernel(x_ref, mem_ref, gc_ref, gm_ref, wq_ref, wk_ref, wv_ref, wo_ref, o_ref, k_sc, v_sc):
    @pl.when(pl.program_id(1) == 0)
    def _():
        mn = _rms(mem_ref[0], gm_ref[...]).astype(BF16)
        k_sc[...] = jnp.dot(mn, wk_ref[...], preferred_element_type=F32).astype(BF16)
        v_sc[...] = jnp.dot(mn, wv_ref[...], preferred_element_type=F32).astype(BF16)

    heads = [slice(hd * XHD, (hd + 1) * XHD) for hd in range(N_XH)]
    rows = [slice(i * (TQ // 2), (i + 1) * (TQ // 2)) for i in range(2)]
    xs = [x_ref[0, r, :] for r in rows]
    qs = [_bdot(_rms(x, gc_ref[...]), wq_ref[...]).astype(BF16) for x in xs]
    scores = [[_bdot_nt(q[:, sl], k_sc[:, sl]) * (XHD ** -0.5) for sl in heads] for q in qs]
    probs = []
    for per_head in scores:
        ps = []
        for s in per_head:
            e = jnp.exp(s - jnp.max(s, axis=-1, keepdims=True))
            ps.append((e / jnp.sum(e, axis=-1, keepdims=True)).astype(BF16))
        probs.append(ps)
    for r, x, ps in zip(rows, xs, probs):
        outs = [jnp.dot(p, v_sc[:, sl], preferred_element_type=F32) for p, sl in zip(ps, heads)]
        o_ref[0, r, :] = x + _bdot(jnp.concatenate(outs, axis=1), wo_ref[...])


def _xattn(x, mem, gc, gm, wq, wk, wv, wo):
    B, S, D = x.shape
    M = mem.shape[1]
    consts = (gc, gm, wq, wk, wv, wo)
    return pl.pallas_call(
        _xattn_kernel,
        grid=(B, S // TQ),
        in_specs=[pl.BlockSpec((1, TQ, D), lambda b, j: (b, j, 0)),
                  pl.BlockSpec((1, M, D), lambda b, j: (b, 0, 0))] + [_const_spec(c.shape) for c in consts],
        out_specs=pl.BlockSpec((1, TQ, D), lambda b, j: (b, j, 0)),
        out_shape=jax.ShapeDtypeStruct((B, S, D), F32),
        scratch_shapes=[pltpu.VMEM((M, D), BF16), pltpu.VMEM((M, D), BF16)],
        compiler_params=pltpu.CompilerParams(
            dimension_semantics=("arbitrary", "arbitrary"), vmem_limit_bytes=VMEM_LIMIT),
        name="xattn",
    )(x, mem, *consts)


def _ffn_kernel(x_ref, gf_ref, w1_ref, w2_ref, gfin_ref, o_ref, *, final_norm):
    rows = [slice(i * (TM // 2), (i + 1) * (TM // 2)) for i in range(2)]
    accs = [x_ref[r, :] for r in rows]
    hs = [_rms(a, gf_ref[...]).astype(BF16) for a in accs]
    for c in range(D_FF // FF_CH):
        sl = slice(c * FF_CH, (c + 1) * FF_CH)
        ts = [jnp.maximum(jnp.dot(h, w1_ref[:, sl], preferred_element_type=F32), 0.0) for h in hs]
        accs = [a + _bdot(t * t, w2_ref[sl, :]) for a, t in zip(accs, ts)]
    for r, a in zip(rows, accs):
        o_ref[r, :] = _rms(a, gfin_ref[...]) if final_norm else a


def _ffn(x2d, gf, w1, w2, gfin, final_norm):
    M, D = x2d.shape
    consts = (gf, w1, w2, gfin)
    return pl.pallas_call(
        functools.partial(_ffn_kernel, final_norm=final_norm),
        grid=(M // TM,),
        in_specs=[pl.BlockSpec((TM, D), lambda i: (i, 0))] + [_const_spec(c.shape) for c in consts],
        out_specs=pl.BlockSpec((TM, D), lambda i: (i, 0)),
        out_shape=jax.ShapeDtypeStruct((M, D), F32),
        compiler_params=pltpu.CompilerParams(
            dimension_semantics=("arbitrary",), vmem_limit_bytes=VMEM_LIMIT),
        name="ffn",
    )(x2d, *consts)


def _row(v):
    return v.reshape(1, -1).astype(F32)


def _mixer_constants():
    i = jnp.arange(TT)
    tri = ((i[:, None] // CH == i[None, :] // CH) & (i[:, None] >= i[None, :])).astype(BF16)
    jj = jnp.arange(GW)
    blk = (jj[:, None] // HEAD == jj[None, :] // HEAD).astype(BF16)
    return tri, blk, blk


def kernel(x, mem, g_mix, w_in, conv_w, conv_b, conv_ln_g, conv_ln_b, mu_b, w0, w_decay2, a0, a_lora2,
           g_lora2, k_k, k_a, r_k, lnx_g, lnx_b, w_out, g_cross, g_mem, wq, wk, wv, wo, g_ffn, w_ff1, w_ff2,
           g_final):
    B, S, D = x.shape
    depth = w_in.shape[0]
    tri, ones_bd, blk = _mixer_constants()
    pad_cols = LORA_PAD - (DEC_L + AAA_L + GATE_L)
    for l in range(depth):
        win = jnp.pad(w_in[l], ((0, 0), (0, pad_cols))).astype(BF16)
        mu = _row(jnp.pad(mu_b[l], (0, pad_cols)))
        wd = jnp.zeros((LORA_PAD, RW), F32).at[0:DEC_L].set(w_decay2[l]).astype(BF16)
        wa = jnp.zeros((LORA_PAD, RW), F32).at[DEC_L:DEC_L + AAA_L].set(a_lora2[l]).astype(BF16)
        wg = jnp.zeros((LORA_PAD, RW), F32).at[DEC_L + AAA_L:DEC_L + AAA_L + GATE_L].set(g_lora2[l]).astype(BF16)
        x = _mixer(x, _row(g_mix[l]), win, conv_w[l].astype(F32), _row(conv_b[l]), _row(conv_ln_g[l]),
                   _row(conv_ln_b[l]), mu, _row(w0[l]), wd, _row(a0[l]), wa, wg, _row(k_k[l]), _row(k_a[l]),
                   _row(r_k[l]), _row(lnx_g[l]), _row(lnx_b[l]), w_out[l].astype(BF16), tri, ones_bd, blk)
        x = _xattn(x, mem, _row(g_cross[l]), _row(g_mem[l]), wq[l].astype(BF16), wk[l].astype(BF16),
                   wv[l].astype(BF16), wo[l].astype(BF16))
        x = _ffn(x.reshape(B * S, D), _row(g_ffn[l]), w_ff1[l].astype(BF16), w_ff2[l].astype(BF16),
                 _row(g_final), final_norm=(l == depth - 1)).reshape(B, S, D)
    return x
```

```python
import functools

import jax
import jax.numpy as jnp
from jax import lax
from jax.experimental import pallas as pl
from jax.experimental.pallas import tpu as pltpu

F32 = jnp.float32
BF16 = jnp.bfloat16

D_MODEL = 1024
CONV_W = 512
CONV_K = 31
RW = 512
HEAD = 64
LORA_PAD = 256
DEC_L, AAA_L, GATE_L = 32, 32, 96
P_COLS = 2 * CONV_W + 3 * RW + LORA_PAD
RWKV_P = 3 * RW + LORA_PAD
N_XH = 4
XHD = D_MODEL // N_XH
D_FF = 4 * D_MODEL
RMS_EPS = 1e-6
LN_EPS = 1e-5
GN_EPS = 1e-5 * HEAD

CH = 64
GRP = 4
GW = GRP * HEAD
TT = 256
NB = 2
A_WIDTH = 8
HALO = 32
CONV_RB = 32
TQ = 512
TM = 1024
FF_CH = 1024
VMEM_LIMIT = 56 * 1024 * 1024


def _bdot(a, b):
    return jnp.dot(a.astype(BF16), b.astype(BF16), preferred_element_type=F32)


def _bdot_nt(a, b):
    return lax.dot_general(a.astype(BF16), b.astype(BF16), (((1,), (1,)), ((), ())),
                           preferred_element_type=F32)


def _rms(x, g):
    return x * lax.rsqrt(jnp.mean(x * x, axis=-1, keepdims=True) + RMS_EPS) * g


def _seg_sum(x, ones_bd):
    hi = x.astype(BF16)
    outs = []
    for g in range(RW // GW):
        sl = slice(g * GW, (g + 1) * GW)
        outs.append(jnp.dot(hi[:, sl], ones_bd, preferred_element_type=F32))
    return jnp.concatenate(outs, axis=1)


def _block_diag(z, blk):
    return jnp.concatenate([z] * GRP, axis=0) * blk


def _interleave(a_chains, b_chains, width):
    pending = list(a_chains)
    live_a, live_b = [], list(b_chains)
    while pending or live_a or live_b:
        while pending and len(live_a) < width:
            live_a.append(pending.pop(0))
        nxt_a, nxt_b = [], []
        for group, nxt in ((live_a, nxt_a), (live_b, nxt_b)):
            for ch in group:
                try:
                    next(ch)
                    nxt.append(ch)
                except StopIteration:
                    pass
        live_a, live_b = nxt_a, nxt_b


def _mixer_kernel(x_ref, gmix_ref, win_ref, cw_ref, cb_ref, clg_ref, clb_ref, mu_ref, w0_ref, wd_ref,
                  a0_ref, wa_ref, wg_ref, kkw_ref, ka_ref, rk_ref, lg_ref, lb_ref, wout_ref,
                  tri_ref, ones_ref, blk_ref,
                  o_ref,
                  p_sc, prev_sc, u_sc, ush_sc, yc_sc, g_sc, bonus_sc, strm_sc, af_sc, ab_sc, bkt_sc, gcol_sc, gam_sc, y_sc,
                  s_sc):
    j = pl.program_id(1)

    @pl.when(j == 0)
    def _():
        prev_sc[...] = jnp.zeros(prev_sc.shape, F32)
        u_sc[:, 0:HALO, :] = jnp.zeros((NB, HALO, CONV_W), F32)
        s_sc[...] = jnp.zeros(s_sc.shape, F32)

    ones_bd = ones_ref[...]
    blk = blk_ref[...]
    blk_f = blk.astype(F32)
    ri = lax.broadcasted_iota(jnp.int32, (CH, GW), 0)
    ci = lax.broadcasted_iota(jnp.int32, (CH, GW), 1) % CH
    tri_s = ri > ci
    tri_i = ri >= ci
    eye_cat = (ri == ci).astype(F32)
    n_grp = RW // GW
    n_chk = TT // CH
    done = set()

    def stage_inproj(bi):
        h = _rms(x_ref[bi], gmix_ref[...])
        p_sc[bi] = _bdot(h, win_ref[...])

    def stage_conv(bi):
        u_sc[bi, HALO:HALO + TT, :] = p_sc[bi, :, 0:CONV_W] * jax.nn.sigmoid(p_sc[bi, :, CONV_W:2 * CONV_W])
        ua = u_sc[bi]
        for sh in range(1, 8):
            ush_sc[bi, sh - 1] = pltpu.roll(ua, HALO + TT - sh, axis=0)
        off = HALO - (CONV_K - 1)
        for rb in range(TT // CONV_RB):
            base = rb * CONV_RB
            acc = jnp.broadcast_to(cb_ref[...], (CONV_RB, CONV_W))
            for t in range(CONV_K):
                m8, sh = divmod(off + t, 8)
                lo = base + 8 * m8
                src = u_sc[bi, lo:lo + CONV_RB, :] if sh == 0 else ush_sc[bi, sh - 1, lo:lo + CONV_RB, :]
                acc = acc + cw_ref[t:t + 1, :] * src
            m = jnp.mean(acc, axis=-1, keepdims=True)
            dlt = acc - m
            var = jnp.mean(dlt * dlt, axis=-1, keepdims=True)
            y_conv = jax.nn.silu(dlt * lax.rsqrt(var + LN_EPS) * clg_ref[...] + clb_ref[...])
            yc_sc[bi, base:base + CONV_RB, :] = y_conv.astype(BF16)
        u_sc[bi, 0:HALO, :] = u_sc[bi, TT:TT + HALO, :]

    def stage_prep(bi):
        pb = p_sc[bi, :, 2 * CONV_W:]
        rolled = pltpu.roll(pb, 1, axis=0)
        first = lax.broadcasted_iota(jnp.int32, (8, RWKV_P), 0) == 0
        prev = jnp.concatenate([jnp.where(first, prev_sc[bi, 0:1, :], rolled[0:8]), rolled[8:]], axis=0)
        prev_sc[bi, 0:1, :] = pb[TT - 1:TT, :]
        z = pb + mu_ref[...] * (prev - pb)
        r = z[:, 0:RW]
        k = z[:, RW:2 * RW]
        v = z[:, 2 * RW:3 * RW]
        zl = z[:, 3 * RW:]
        dec = _bdot(jnp.tanh(zl), wd_ref[...])
        w_log = -jax.nn.softplus(-(w0_ref[...] + dec)) - 0.5
        lw = -jnp.exp(w_log)
        a = jax.nn.sigmoid(a0_ref[...] + _bdot(zl, wa_ref[...]))
        g_sc[bi] = _bdot(jax.nn.sigmoid(zl), wg_ref[...])
        kk = k * kkw_ref[...]
        kk = kk / jnp.maximum(jnp.sqrt(_seg_sum(kk * kk, ones_bd)), 1e-12)
        k = k * (1.0 + (a - 1.0) * ka_ref[...])
        bonus_sc[bi] = _seg_sum(r * k * rk_ref[...], ones_bd) * v
        l1 = lw.astype(BF16)
        l2 = (lw - l1.astype(F32)).astype(BF16)
        tri = tri_ref[...]
        cs = jnp.dot(tri, l1, preferred_element_type=F32) + jnp.dot(tri, l2, preferred_element_type=F32)
        cl = jnp.concatenate(
            [jnp.broadcast_to(cs[c * CH + CH - 1:c * CH + CH, :], (CH, RW)) for c in range(n_chk)], axis=0)
        e_in = jnp.exp(cs)
        e_inv = jnp.exp(-cs)
        e_end = jnp.exp(cl - cs)
        kka = kk * a
        strm_sc[bi, 0] = (kk * jnp.exp(cs - lw)).astype(BF16)
        strm_sc[bi, 1] = (r * e_in).astype(BF16)
        strm_sc[bi, 2] = (kka * e_inv).astype(BF16)
        strm_sc[bi, 3] = (k * e_inv).astype(BF16)
        strm_sc[bi, 4] = (kka * e_end).astype(BF16)
        strm_sc[bi, 5] = (k * e_end).astype(BF16)
        strm_sc[bi, 6] = v.astype(BF16)
        gam_sc[bi] = jnp.exp(cl)

    def chain_a(bi, c, gi):
        rows = slice(c * CH, (c + 1) * CH)
        ln = slice(gi * GW, (gi + 1) * GW)
        kt = strm_sc[bi, 0, rows, ln]
        lhs = jnp.concatenate([kt, strm_sc[bi, 1, rows, ln]], axis=0)
        gb = _bdot_nt(lhs, _block_diag(strm_sc[bi, 2, rows, ln], blk))
        gk = _bdot_nt(lhs, _block_diag(strm_sc[bi, 3, rows, ln], blk))
        yield
        pm = jnp.where(tri_s, -gb[:CH], 0.0)
        lk = jnp.where(tri_s, gk[:CH], 0.0)
        ab_sc[bi, 1, rows, ln] = jnp.where(tri_i, gb[CH:], 0.0).astype(BF16)
        ark = jnp.where(tri_i, gk[CH:], 0.0)
        w = eye_cat + pm
        q = _bdot(pm, _block_diag(pm.astype(BF16), blk))
        yield
        for _ in range(4):
            rr = _bdot(jnp.concatenate([w, q], axis=0), _block_diag(q.astype(BF16), blk))
            yield
            w = w + rr[:CH]
            q = rr[CH:]
        wq = _bdot(w, _block_diag(q.astype(BF16), blk))
        xy = _bdot(jnp.concatenate([lk, ark], axis=0), _block_diag(strm_sc[bi, 6, rows, ln], blk))
        yield
        wb = (w + wq).astype(BF16)
        af_sc[bi, 1, rows, ln] = xy[CH:]
        u_ = _bdot(wb, _block_diag(xy[:CH].astype(BF16), blk))
        wk = _bdot(wb, _block_diag(kt, blk))
        bk = jnp.concatenate([strm_sc[bi, 4, rows, ln], strm_sc[bi, 5, rows, ln]], axis=0).astype(F32)
        bkt_sc[bi, c, gi] = bk.T.astype(BF16)
        gcol_sc[bi, c, gi] = jnp.broadcast_to(gam_sc[bi, c * CH:c * CH + 1, ln], (GW, GW)).T
        yield
        af_sc[bi, 0, rows, ln] = -u_
        ab_sc[bi, 0, rows, ln] = (-wk).astype(BF16)
        done.add((bi, c, gi))

    def chain_b(bi, gi):
        ln = slice(gi * GW, (gi + 1) * GW)
        st = s_sc[bi, gi]
        for c in range(n_chk):
            while (bi, c, gi) not in done:
                yield
            rows = slice(c * CH, (c + 1) * CH)
            zz = _bdot(jnp.concatenate([ab_sc[bi, 0, rows, ln], strm_sc[bi, 1, rows, ln]], axis=0), st)
            yield
            dmb = (af_sc[bi, 0, rows, ln] + zz[:CH]).astype(BF16)
            upd = jnp.dot(bkt_sc[bi, c, gi], jnp.concatenate([dmb, strm_sc[bi, 6, rows, ln]], axis=0),
                          preferred_element_type=F32)
            yd = _bdot(ab_sc[bi, 1, rows, ln], _block_diag(dmb, blk))
            yield
            st = gcol_sc[bi, c, gi] * st + upd * blk_f
            y_sc[bi, rows, ln] = zz[CH:] + af_sc[bi, 1, rows, ln] + yd
        s_sc[bi, gi] = st

    def stage_tail(bi):
        yw = y_sc[bi]
        mu_h = _seg_sum(yw, ones_bd) * (1.0 / HEAD)
        dy = yw - mu_h
        var_h = _seg_sum(dy * dy, ones_bd) * (1.0 / HEAD)
        yn = dy * lax.rsqrt(var_h + GN_EPS) * lg_ref[...] + lb_ref[...]
        y_rwkv = (yn + bonus_sc[bi]) * g_sc[bi]
        o_ref[bi] = (x_ref[bi] + jnp.dot(yc_sc[bi], wout_ref[0:CONV_W, :], preferred_element_type=F32)
                     + _bdot(y_rwkv, wout_ref[CONV_W:, :]))

    for bi in range(NB):
        stage_inproj(bi)
    for bi in range(NB):
        stage_conv(bi)
        stage_prep(bi)
    a_chains = [chain_a(bi, c, gi) for c in range(n_chk) for bi in range(NB) for gi in range(n_grp)]
    b_chains = [chain_b(bi, gi) for bi in range(NB) for gi in range(n_grp)]
    _interleave(a_chains, b_chains, A_WIDTH)
    for bi in range(NB):
        stage_tail(bi)


def _const_spec(shape):
    nd = len(shape)
    return pl.BlockSpec(shape, lambda *_: (0,) * nd, pipeline_mode=pl.Buffered(1))


def _mixer(x, gmix, win, cw, cb, clg, clb, mu, w0, wd, a0, wa, wg, kkw, ka, rk, lg, lb, wout, tri, ones_bd, blk):
    B, S, D = x.shape
    consts = (gmix, win, cw, cb, clg, clb, mu, w0, wd, a0, wa, wg, kkw, ka, rk, lg, lb, wout, tri, ones_bd, blk)
    return pl.pallas_call(
        _mixer_kernel,
        grid=(B // NB, S // TT),
        in_specs=[pl.BlockSpec((NB, TT, D), lambda b, j: (b, j, 0))] + [_const_spec(c.shape) for c in consts],
        out_specs=pl.BlockSpec((NB, TT, D), lambda b, j: (b, j, 0)),
        out_shape=jax.ShapeDtypeStruct((B, S, D), F32),
        scratch_shapes=[
            pltpu.VMEM((NB, TT, P_COLS), F32),
            pltpu.VMEM((NB, 8, RWKV_P), F32),
            pltpu.VMEM((NB, HALO + TT, CONV_W), F32),
            pltpu.VMEM((NB, 7, HALO + TT, CONV_W), F32),
            pltpu.VMEM((NB, TT, CONV_W), BF16),
            pltpu.VMEM((NB, TT, RW), F32),
            pltpu.VMEM((NB, TT, RW), F32),
            pltpu.VMEM((NB, 7, TT, RW), BF16),
            pltpu.VMEM((NB, 2, TT, RW), F32),
            pltpu.VMEM((NB, 2, TT, RW), BF16),
            pltpu.VMEM((NB, TT // CH, RW // GW, GW, 2 * CH), BF16),
            pltpu.VMEM((NB, TT // CH, RW // GW, GW, GW), F32),
            pltpu.VMEM((NB, TT, RW), F32),
            pltpu.VMEM((NB, TT, RW), F32),
            pltpu.VMEM((NB, RW // GW, GW, GW), F32),
        ],
        compiler_params=pltpu.CompilerParams(
            dimension_semantics=("arbitrary", "arbitrary"), vmem_limit_bytes=VMEM_LIMIT),
        name="mixer",
    )(x, *consts)


def _xattn_kernel(x_ref, mem_ref, gc_ref, gm_ref, wq_ref, wk_ref, wv_ref, wo_ref, o_ref, k_sc, v_sc):
    @pl.when(pl.program_id(1) == 0)
    def _():
        mn = _rms(mem_ref[0], gm_ref[...]).astype(BF16)
        k_sc[...] = jnp.dot(mn, wk_ref[...], preferred_element_type=F32).astype(BF16)
        v_sc[...] = jnp.dot(mn, wv_ref[...], preferred_element_type=F32).astype(BF16)

    heads = [slice(hd * XHD, (hd + 1) * XHD) for hd in range(N_XH)]
    rows = [slice(i * (TQ // 2), (i + 1) * (TQ // 2)) for i in range(2)]
    xs = [x_ref[0, r, :] for r in rows]
    qs = [_bdot(_rms(x, gc_ref[...]), wq_ref[...]).astype(BF16) for x in xs]
    scores = [[_bdot_nt(q[:, sl], k_sc[:, sl]) * (XHD ** -0.5) for sl in heads] for q in qs]
    probs = []
    for per_head in scores:
        ps = []
        for s in per_head:
            e = jnp.exp(s - jnp.max(s, axis=-1, keepdims=True))
            ps.append((e / jnp.sum(e, axis=-1, keepdims=True)).astype(BF16))
        probs.append(ps)
    for r, x, ps in zip(rows, xs, probs):
        outs = [jnp.dot(p, v_sc[:, sl], preferred_element_type=F32) for p, sl in zip(ps, heads)]
        o_ref[0, r, :] = x + _bdot(jnp.concatenate(outs, axis=1), wo_ref[...])


def _xattn(x, mem, gc, gm, wq, wk, wv, wo):
    B, S, D = x.shape
    M = mem.shape[1]
    consts = (gc, gm, wq, wk, wv, wo)
    return pl.pallas_call(
        _xattn_kernel,
        grid=(B, S // TQ),
        in_specs=[pl.BlockSpec((1, TQ, D), lambda b, j: (b, j, 0)),
                  pl.BlockSpec((1, M, D), lambda b, j: (b, 0, 0))] + [_const_spec(c.shape) for c in consts],
        out_specs=pl.BlockSpec((1, TQ, D), lambda b, j: (b, j, 0)),
        out_shape=jax.ShapeDtypeStruct((B, S, D), F32),
        scratch_shapes=[pltpu.VMEM((M, D), BF16), pltpu.VMEM((M, D), BF16)],
        compiler_params=pltpu.CompilerParams(
            dimension_semantics=("arbitrary", "arbitrary"), vmem_limit_bytes=VMEM_LIMIT),
        name="xattn",
    )(x, mem, *consts)


def _ffn_kernel(x_ref, gf_ref, w1_ref, w2_ref, gfin_ref, o_ref, *, final_norm):
    rows = [slice(i * (TM // 2), (i + 1) * (TM // 2)) for i in range(2)]
    accs = [x_ref[r, :] for r in rows]
    hs = [_rms(a, gf_ref[...]).astype(BF16) for a in accs]
    for c in range(D_FF // FF_CH):
        sl = slice(c * FF_CH, (c + 1) * FF_CH)
        ts = [jnp.maximum(jnp.dot(h, w1_ref[:, sl], preferred_element_type=F32), 0.0) for h in hs]
        accs = [a + _bdot(t * t, w2_ref[sl, :]) for a, t in zip(accs, ts)]
    for r, a in zip(rows, accs):
        o_ref[r, :] = _rms(a, gfin_ref[...]) if final_norm else a


def _ffn(x2d, gf, w1, w2, gfin, final_norm):
    M, D = x2d.shape
    consts = (gf, w1, w2, gfin)
    return pl.pallas_call(
        functools.partial(_ffn_kernel, final_norm=final_norm),
        grid=(M // TM,),
        in_specs=[pl.BlockSpec((TM, D), lambda i: (i, 0))] + [_const_spec(c.shape) for c in consts],
        out_specs=pl.BlockSpec((TM, D), lambda i: (i, 0)),
        out_shape=jax.ShapeDtypeStruct((M, D), F32),
        compiler_params=pltpu.CompilerParams(
            dimension_semantics=("arbitrary",), vmem_limit_bytes=VMEM_LIMIT),
        name="ffn",
    )(x2d, *consts)


def _row(v):
    return v.reshape(1, -1).astype(F32)


def _mixer_constants():
    i = jnp.arange(TT)
    tri = ((i[:, None] // CH == i[None, :] // CH) & (i[:, None] >= i[None, :])).astype(BF16)
    jj = jnp.arange(GW)
    blk = (jj[:, None] // HEAD == jj[None, :] // HEAD).astype(BF16)
    return tri, blk, blk


def kernel(x, mem, g_mix, w_in, conv_w, conv_b, conv_ln_g, conv_ln_b, mu_b, w0, w_decay2, a0, a_lora2,
           g_lora2, k_k, k_a, r_k, lnx_g, lnx_b, w_out, g_cross, g_mem, wq, wk, wv, wo, g_ffn, w_ff1, w_ff2,
           g_final):
    B, S, D = x.shape
    depth = w_in.shape[0]
    tri, ones_bd, blk = _mixer_constants()
    pad_cols = LORA_PAD - (DEC_L + AAA_L + GATE_L)
    for l in range(depth):
        win = jnp.pad(w_in[l], ((0, 0), (0, pad_cols))).astype(BF16)
        mu = _row(jnp.pad(mu_b[l], (0, pad_cols)))
        wd = jnp.zeros((LORA_PAD, RW), F32).at[0:DEC_L].set(w_decay2[l]).astype(BF16)
        wa = jnp.zeros((LORA_PAD, RW), F32).at[DEC_L:DEC_L + AAA_L].set(a_lora2[l]).astype(BF16)
        wg = jnp.zeros((LORA_PAD, RW), F32).at[DEC_L + AAA_L:DEC_L + AAA_L + GATE_L].set(g_lora2[l]).astype(BF16)
        x = _mixer(x, _row(g_mix[l]), win, conv_w[l].astype(F32), _row(conv_b[l]), _row(conv_ln_g[l]),
                   _row(conv_ln_b[l]), mu, _row(w0[l]), wd, _row(a0[l]), wa, wg, _row(k_k[l]), _row(k_a[l]),
                   _row(r_k[l]), _row(lnx_g[l]), _row(lnx_b[l]), w_out[l].astype(BF16), tri, ones_bd, blk)
        x = _xattn(x, mem, _row(g_cross[l]), _row(g_mem[l]), wq[l].astype(BF16), wk[l].astype(BF16),
                   wv[l].astype(BF16), wo[l].astype(BF16))
        x = _ffn(x.reshape(B * S, D), _row(g_ffn[l]), w_ff1[l].astype(BF16), w_ff2[l].astype(BF16),
                 _row(g_final), final_norm=(l == depth - 1)).reshape(B, S, D)
    return x
```

```python
import functools

import jax
import jax.numpy as jnp
from jax import lax
from jax.experimental import pallas as pl
from jax.experimental.pallas import tpu as pltpu

F32 = jnp.float32
BF16 = jnp.bfloat16

D_MODEL = 1024
CONV_W = 512
CONV_K = 31
RW = 512
HEAD = 64
LORA_PAD = 256
DEC_L, AAA_L, GATE_L = 32, 32, 96
P_COLS = 2 * CONV_W + 3 * RW + LORA_PAD
RWKV_P = 3 * RW + LORA_PAD
N_XH = 4
XHD = D_MODEL // N_XH
D_FF = 4 * D_MODEL
RMS_EPS = 1e-6
LN_EPS = 1e-5
GN_EPS = 1e-5 * HEAD

CH = 64
GRP = 4
GW = GRP * HEAD
TT = 256
NB = 2
A_WIDTH = 8
HALO = 32
CONV_RB = 32
TQ = 512
TM = 1024
FF_CH = 1024
VMEM_LIMIT = 56 * 1024 * 1024


def _bdot(a, b):
    return jnp.dot(a.astype(BF16), b.astype(BF16), preferred_element_type=F32)


def _bdot_nt(a, b):
    return lax.dot_general(a.astype(BF16), b.astype(BF16), (((1,), (1,)), ((), ())),
                           preferred_element_type=F32)


def _rms(x, g):
    return x * lax.rsqrt(jnp.mean(x * x, axis=-1, keepdims=True) + RMS_EPS) * g


def _seg_sum(x, ones_bd):
    hi = x.astype(BF16)
    outs = []
    for g in range(RW // GW):
        sl = slice(g * GW, (g + 1) * GW)
        outs.append(jnp.dot(hi[:, sl], ones_bd[...], preferred_element_type=F32))
    return jnp.concatenate(outs, axis=1)


def _block_diag(z, blk):
    return jnp.concatenate([z] * GRP, axis=0) * blk[...]


def _interleave(a_chains, b_chains, width):
    pending = list(a_chains)
    live_a, live_b = [], list(b_chains)
    while pending or live_a or live_b:
        while pending and len(live_a) < width:
            live_a.append(pending.pop(0))
        nxt_a, nxt_b = [], []
        for group, nxt in ((live_a, nxt_a), (live_b, nxt_b)):
            for ch in group:
                try:
                    next(ch)
                    nxt.append(ch)
                except StopIteration:
                    pass
        live_a, live_b = nxt_a, nxt_b


def _mixer_kernel(x_ref, gmix_ref, win_ref, cw_ref, cb_ref, clg_ref, clb_ref, mu_ref, w0_ref, wd_ref,
                  a0_ref, wa_ref, wg_ref, kkw_ref, ka_ref, rk_ref, lg_ref, lb_ref, wout_ref,
                  tri_ref, ones_ref, blk_ref,
                  o_ref,
                  p_sc, prev_sc, u_sc, ush_sc, yc_sc, g_sc, bonus_sc, strm_sc, af_sc, ab_sc, bkt_sc, gcol_sc, gam_sc, y_sc,
                  s_sc):
    j = pl.program_id(1)

    @pl.when(j == 0)
    def _():
        prev_sc[...] = jnp.zeros(prev_sc.shape, F32)
        u_sc[:, 0:HALO, :] = jnp.zeros((NB, HALO, CONV_W), F32)
        s_sc[...] = jnp.zeros(s_sc.shape, F32)

    ones_bd = ones_ref
    blk = blk_ref
    blk_f = blk_ref[...].astype(F32)
    ri = lax.broadcasted_iota(jnp.int32, (CH, GW), 0)
    ci = lax.broadcasted_iota(jnp.int32, (CH, GW), 1) % CH
    tri_s = ri > ci
    tri_i = ri >= ci
    eye_cat = (ri == ci).astype(F32)
    n_grp = RW // GW
    n_chk = TT // CH
    done = set()

    def stage_inproj(bi):
        h = _rms(x_ref[bi], gmix_ref[...])
        p_sc[bi] = _bdot(h, win_ref[...])

    def stage_conv(bi):
        u_sc[bi, HALO:HALO + TT, :] = p_sc[bi, :, 0:CONV_W] * jax.nn.sigmoid(p_sc[bi, :, CONV_W:2 * CONV_W])
        ua = u_sc[bi]
        for sh in range(1, 8):
            ush_sc[bi, sh - 1] = pltpu.roll(ua, HALO + TT - sh, axis=0)
        off = HALO - (CONV_K - 1)
        for rb in range(TT // CONV_RB):
            base = rb * CONV_RB
            acc = jnp.broadcast_to(cb_ref[...], (CONV_RB, CONV_W))
            for t in range(CONV_K):
                m8, sh = divmod(off + t, 8)
                lo = base + 8 * m8
                src = u_sc[bi, lo:lo + CONV_RB, :] if sh == 0 else ush_sc[bi, sh - 1, lo:lo + CONV_RB, :]
                acc = acc + cw_ref[t:t + 1, :] * src
            m = jnp.mean(acc, axis=-1, keepdims=True)
            dlt = acc - m
            var = jnp.mean(dlt * dlt, axis=-1, keepdims=True)
            y_conv = jax.nn.silu(dlt * lax.rsqrt(var + LN_EPS) * clg_ref[...] + clb_ref[...])
            yc_sc[bi, base:base + CONV_RB, :] = y_conv.astype(BF16)
        u_sc[bi, 0:HALO, :] = u_sc[bi, TT:TT + HALO, :]

    def stage_prep(bi):
        pb = p_sc[bi, :, 2 * CONV_W:]
        rolled = pltpu.roll(pb, 1, axis=0)
        first = lax.broadcasted_iota(jnp.int32, (8, RWKV_P), 0) == 0
        prev = jnp.concatenate([jnp.where(first, prev_sc[bi, 0:1, :], rolled[0:8]), rolled[8:]], axis=0)
        prev_sc[bi, 0:1, :] = pb[TT - 1:TT, :]
        z = pb + mu_ref[...] * (prev - pb)
        r = z[:, 0:RW]
        k = z[:, RW:2 * RW]
        v = z[:, 2 * RW:3 * RW]
        zl = z[:, 3 * RW:]
        dec = _bdot(jnp.tanh(zl), wd_ref[...])
        w_log = -jax.nn.softplus(-(w0_ref[...] + dec)) - 0.5
        lw = -jnp.exp(w_log)
        a = jax.nn.sigmoid(a0_ref[...] + _bdot(zl, wa_ref[...]))
        g_sc[bi] = _bdot(jax.nn.sigmoid(zl), wg_ref[...])
        kk = k * kkw_ref[...]
        kk = kk / jnp.maximum(jnp.sqrt(_seg_sum(kk * kk, ones_bd)), 1e-12)
        k = k * (1.0 + (a - 1.0) * ka_ref[...])
        bonus_sc[bi] = _seg_sum(r * k * rk_ref[...], ones_bd) * v
        l1 = lw.astype(BF16)
        l2 = (lw - l1.astype(F32)).astype(BF16)
        tri = tri_ref[...]
        cs = jnp.dot(tri, l1, preferred_element_type=F32) + jnp.dot(tri, l2, preferred_element_type=F32)
        cl = jnp.concatenate(
            [jnp.broadcast_to(cs[c * CH + CH - 1:c * CH + CH, :], (CH, RW)) for c in range(n_chk)], axis=0)
        e_in = jnp.exp(cs)
        e_inv = jnp.exp(-cs)
        e_end = jnp.exp(cl - cs)
        kka = kk * a
        strm_sc[bi, 0] = (kk * jnp.exp(cs - lw)).astype(BF16)
        strm_sc[bi, 1] = (r * e_in).astype(BF16)
        strm_sc[bi, 2] = (kka * e_inv).astype(BF16)
        strm_sc[bi, 3] = (k * e_inv).astype(BF16)
        strm_sc[bi, 4] = (kka * e_end).astype(BF16)
        strm_sc[bi, 5] = (k * e_end).astype(BF16)
        strm_sc[bi, 6] = v.astype(BF16)
        gam_sc[bi] = jnp.exp(cl)

    def chain_a(bi, c, gi):
        rows = slice(c * CH, (c + 1) * CH)
        ln = slice(gi * GW, (gi + 1) * GW)
        kt = strm_sc[bi, 0, rows, ln]
        lhs = jnp.concatenate([kt, strm_sc[bi, 1, rows, ln]], axis=0)
        gb = _bdot_nt(lhs, _block_diag(strm_sc[bi, 2, rows, ln], blk))
        gk = _bdot_nt(lhs, _block_diag(strm_sc[bi, 3, rows, ln], blk))
        yield
        pm = jnp.where(tri_s, -gb[:CH], 0.0)
        lk = jnp.where(tri_s, gk[:CH], 0.0)
        ab_sc[bi, 1, rows, ln] = jnp.where(tri_i, gb[CH:], 0.0).astype(BF16)
        ark = jnp.where(tri_i, gk[CH:], 0.0)
        w = eye_cat + pm
        q = _bdot(pm, _block_diag(pm.astype(BF16), blk))
        yield
        for _ in range(4):
            rr = _bdot(jnp.concatenate([w, q], axis=0), _block_diag(q.astype(BF16), blk))
            yield
            w = w + rr[:CH]
            q = rr[CH:]
        wq = _bdot(w, _block_diag(q.astype(BF16), blk))
        xy = _bdot(jnp.concatenate([lk, ark], axis=0), _block_diag(strm_sc[bi, 6, rows, ln], blk))
        yield
        wb = (w + wq).astype(BF16)
        af_sc[bi, 1, rows, ln] = xy[CH:]
        u_ = _bdot(wb, _block_diag(xy[:CH].astype(BF16), blk))
        wk = _bdot(wb, _block_diag(kt, blk))
        bk = jnp.concatenate([strm_sc[bi, 4, rows, ln], strm_sc[bi, 5, rows, ln]], axis=0).astype(F32)
        bkt_sc[bi, c, gi] = bk.T.astype(BF16)
        gcol_sc[bi, c, gi] = jnp.broadcast_to(gam_sc[bi, c * CH:c * CH + 1, ln], (GW, GW)).T
        yield
        af_sc[bi, 0, rows, ln] = -u_
        ab_sc[bi, 0, rows, ln] = (-wk).astype(BF16)
        done.add((bi, c, gi))

    def chain_b(bi, gi):
        ln = slice(gi * GW, (gi + 1) * GW)
        st = s_sc[bi, gi]
        for c in range(n_chk):
            while (bi, c, gi) not in done:
                yield
            rows = slice(c * CH, (c + 1) * CH)
            zz = _bdot(jnp.concatenate([ab_sc[bi, 0, rows, ln], strm_sc[bi, 1, rows, ln]], axis=0), st)
            yield
            dmb = (af_sc[bi, 0, rows, ln] + zz[:CH]).astype(BF16)
            upd = jnp.dot(bkt_sc[bi, c, gi], jnp.concatenate([dmb, strm_sc[bi, 6, rows, ln]], axis=0),
                          preferred_element_type=F32)
            yd = _bdot(ab_sc[bi, 1, rows, ln], _block_diag(dmb, blk))
            yield
            st = gcol_sc[bi, c, gi] * st + upd * blk_f
            y_sc[bi, rows, ln] = zz[CH:] + af_sc[bi, 1, rows, ln] + yd
        s_sc[bi, gi] = st

    def stage_tail(bi):
        yw = y_sc[bi]
        mu_h = _seg_sum(yw, ones_bd) * (1.0 / HEAD)
        dy = yw - mu_h
        var_h = _seg_sum(dy * dy, ones_bd) * (1.0 / HEAD)
        yn = dy * lax.rsqrt(var_h + GN_EPS) * lg_ref[...] + lb_ref[...]
        y_rwkv = (yn + bonus_sc[bi]) * g_sc[bi]
        o_ref[bi] = (x_ref[bi] + jnp.dot(yc_sc[bi], wout_ref[0:CONV_W, :], preferred_element_type=F32)
                     + _bdot(y_rwkv, wout_ref[CONV_W:, :]))

    for bi in range(NB):
        stage_inproj(bi)
    for bi in range(NB):
        stage_conv(bi)
        stage_prep(bi)
    a_chains = [chain_a(bi, c, gi) for c in range(n_chk) for bi in range(NB) for gi in range(n_grp)]
    b_chains = [chain_b(bi, gi) for bi in range(NB) for gi in range(n_grp)]
    _interleave(a_chains, b_chains, A_WIDTH)
    for bi in range(NB):
        stage_tail(bi)


def _const_spec(shape):
    nd = len(shape)
    return pl.BlockSpec(shape, lambda *_: (0,) * nd, pipeline_mode=pl.Buffered(1))


def _mixer(x, gmix, win, cw, cb, clg, clb, mu, w0, wd, a0, wa, wg, kkw, ka, rk, lg, lb, wout, tri, ones_bd, blk):
    B, S, D = x.shape
    consts = (gmix, win, cw, cb, clg, clb, mu, w0, wd, a0, wa, wg, kkw, ka, rk, lg, lb, wout, tri, ones_bd, blk)
    return pl.pallas_call(
        _mixer_kernel,
        grid=(B // NB, S // TT),
        in_specs=[pl.BlockSpec((NB, TT, D), lambda b, j: (b, j, 0))] + [_const_spec(c.shape) for c in consts],
        out_specs=pl.BlockSpec((NB, TT, D), lambda b, j: (b, j, 0)),
        out_shape=jax.ShapeDtypeStruct((B, S, D), F32),
        scratch_shapes=[
            pltpu.VMEM((NB, TT, P_COLS), F32),
            pltpu.VMEM((NB, 8, RWKV_P), F32),
            pltpu.VMEM((NB, HALO + TT, CONV_W), F32),
            pltpu.VMEM((NB, 7, HALO + TT, CONV_W), F32),
            pltpu.VMEM((NB, TT, CONV_W), BF16),
            pltpu.VMEM((NB, TT, RW), F32),
            pltpu.VMEM((NB, TT, RW), F32),
            pltpu.VMEM((NB, 7, TT, RW), BF16),
            pltpu.VMEM((NB, 2, TT, RW), F32),
            pltpu.VMEM((NB, 2, TT, RW), BF16),
            pltpu.VMEM((NB, TT // CH, RW // GW, GW, 2 * CH), BF16),
            pltpu.VMEM((NB, TT // CH, RW // GW, GW, GW), F32),
            pltpu.VMEM((NB, TT, RW), F32),
            pltpu.VMEM((NB, TT, RW), F32),
            pltpu.VMEM((NB, RW // GW, GW, GW), F32),
        ],
        compiler_params=pltpu.CompilerParams(
            dimension_semantics=("arbitrary", "arbitrary"), vmem_limit_bytes=VMEM_LIMIT),
        name="mixer",
    )(x, *consts)


def _xattn_kernel(x_ref, mem_ref, gc_ref, gm_ref, wq_ref, wk_ref, wv_ref, wo_ref, o_ref, k_sc, v_sc):
    @pl.when(pl.program_id(1) == 0)
    def _():
        mn = _rms(mem_ref[0], gm_ref[...]).astype(BF16)
        k_sc[...] = jnp.dot(mn, wk_ref[...], preferred_element_type=F32).astype(BF16)
        v_sc[...] = jnp.dot(mn, wv_ref[...], preferred_element_type=F32).astype(BF16)

    heads = [slice(hd * XHD, (hd + 1) * XHD) for hd in range(N_XH)]
    rows = [slice(i * (TQ // 2), (i + 1) * (TQ // 2)) for i in range(2)]
    xs = [x_ref[0, r, :] for r in rows]
    qs = [_bdot(_rms(x, gc_ref[...]), wq_ref[...]).astype(BF16) for x in xs]
    scores = [[_bdot_nt(q[:, sl], k_sc[:, sl]) * (XHD ** -0.5) for sl in heads] for q in qs]
    probs = []
    for per_head in scores:
        ps = []
        for s in per_head:
            e = jnp.exp(s - jnp.max(s, axis=-1, keepdims=True))
            ps.append((e / jnp.sum(e, axis=-1, keepdims=True)).astype(BF16))
        probs.append(ps)
    for r, x, ps in zip(rows, xs, probs):
        outs = [jnp.dot(p, v_sc[:, sl], preferred_element_type=F32) for p, sl in zip(ps, heads)]
        o_ref[0, r, :] = x + _bdot(jnp.concatenate(outs, axis=1), wo_ref[...])


def _xattn(x, mem, gc, gm, wq, wk, wv, wo):
    B, S, D = x.shape
    M = mem.shape[1]
    consts = (gc, gm, wq, wk, wv, wo)
    return pl.pallas_call(
        _xattn_kernel,
        grid=(B, S // TQ),
        in_specs=[pl.BlockSpec((1, TQ, D), lambda b, j: (b, j, 0)),
                  pl.BlockSpec((1, M, D), lambda b, j: (b, 0, 0))] + [_const_spec(c.shape) for c in consts],
        out_specs=pl.BlockSpec((1, TQ, D), lambda b, j: (b, j, 0)),
        out_shape=jax.ShapeDtypeStruct((B, S, D), F32),
        scratch_shapes=[pltpu.VMEM((M, D), BF16), pltpu.VMEM((M, D), BF16)],
        compiler_params=pltpu.CompilerParams(
            dimension_semantics=("arbitrary", "arbitrary"), vmem_limit_bytes=VMEM_LIMIT),
        name="xattn",
    )(x, mem, *consts)


def _ffn_kernel(x_ref, gf_ref, w1_ref, w2_ref, gfin_ref, o_ref, *, final_norm):
    rows = [slice(i * (TM // 2), (i + 1) * (TM // 2)) for i in range(2)]
    accs = [x_ref[r, :] for r in rows]
    hs = [_rms(a, gf_ref[...]).astype(BF16) for a in accs]
    for c in range(D_FF // FF_CH):
        sl = slice(c * FF_CH, (c + 1) * FF_CH)
        ts = [jnp.maximum(jnp.dot(h, w1_ref[:, sl], preferred_element_type=F32), 0.0) for h in hs]
        accs = [a + _bdot(t * t, w2_ref[sl, :]) for a, t in zip(accs, ts)]
    for r, a in zip(rows, accs):
        o_ref[r, :] = _rms(a, gfin_ref[...]) if final_norm else a


def _ffn(x2d, gf, w1, w2, gfin, final_norm):
    M, D = x2d.shape
    consts = (gf, w1, w2, gfin)
    return pl.pallas_call(
        functools.partial(_ffn_kernel, final_norm=final_norm),
        grid=(M // TM,),
        in_specs=[pl.BlockSpec((TM, D), lambda i: (i, 0))] + [_const_spec(c.shape) for c in consts],
        out_specs=pl.BlockSpec((TM, D), lambda i: (i, 0)),
        out_shape=jax.ShapeDtypeStruct((M, D), F32),
        compiler_params=pltpu.CompilerParams(
            dimension_semantics=("arbitrary",), vmem_limit_bytes=VMEM_LIMIT),
        name="ffn",
    )(x2d, *consts)


def _row(v):
    return v.reshape(1, -1).astype(F32)


def _mixer_constants():
    i = jnp.arange(TT)
    tri = ((i[:, None] // CH == i[None, :] // CH) & (i[:, None] >= i[None, :])).astype(BF16)
    jj = jnp.arange(GW)
    blk = (jj[:, None] // HEAD == jj[None, :] // HEAD).astype(BF16)
    return tri, blk, blk


def kernel(x, mem, g_mix, w_in, conv_w, conv_b, conv_ln_g, conv_ln_b, mu_b, w0, w_decay2, a0, a_lora2,
           g_lora2, k_k, k_a, r_k, lnx_g, lnx_b, w_out, g_cross, g_mem, wq, wk, wv, wo, g_ffn, w_ff1, w_ff2,
           g_final):
    B, S, D = x.shape
    depth = w_in.shape[0]
    tri, ones_bd, blk = _mixer_constants()
    pad_cols = LORA_PAD - (DEC_L + AAA_L + GATE_L)
    for l in range(depth):
        win = jnp.pad(w_in[l], ((0, 0), (0, pad_cols))).astype(BF16)
        mu = _row(jnp.pad(mu_b[l], (0, pad_cols)))
        wd = jnp.zeros((LORA_PAD, RW), F32).at[0:DEC_L].set(w_decay2[l]).astype(BF16)
        wa = jnp.zeros((LORA_PAD, RW), F32).at[DEC_L:DEC_L + AAA_L].set(a_lora2[l]).astype(BF16)
        wg = jnp.zeros((LORA_PAD, RW), F32).at[DEC_L + AAA_L:DEC_L + AAA_L + GATE_L].set(g_lora2[l]).astype(BF16)
        x = _mixer(x, _row(g_mix[l]), win, conv_w[l].astype(F32), _row(conv_b[l]), _row(conv_ln_g[l]),
                   _row(conv_ln_b[l]), mu, _row(w0[l]), wd, _row(a0[l]), wa, wg, _row(k_k[l]), _row(k_a[l]),
                   _row(r_k[l]), _row(lnx_g[l]), _row(lnx_b[l]), w_out[l].astype(BF16), tri, ones_bd, blk)
        x = _xattn(x, mem, _row(g_cross[l]), _row(g_mem[l]), wq[l].astype(BF16), wk[l].astype(BF16),
                   wv[l].astype(BF16), wo[l].astype(BF16))
        x = _ffn(x.reshape(B * S, D), _row(g_ffn[l]), w_ff1[l].astype(BF16), w_ff2[l].astype(BF16),
                 _row(g_final), final_norm=(l == depth - 1)).reshape(B, S, D)
    return x
```
